```python
import math
import jax, jax.numpy as jnp
from jax import lax
import numpy as np

D_MODEL = 1024
BATCH = 8
SEQ = 2048
DEPTH = 2

CHUNK = 64
Q_BLOCK = 128
EPS = 1e-6
N_A = (DEPTH + 1) // 2
N_B = DEPTH - N_A
D_FF = 2816
S5_GROUP = 16
S5_GROUPS = D_MODEL // S5_GROUP
S5_STATE = 64
S5_DT_MIN = 1e-3
S5_DT_MAX = 1e-1
MLA_HEADS = 8
NOPE_DIM = 128
ROPE_DIM = 64
V_DIM = 128
Q_LORA = 256
KV_LORA = 128
ROPE_BASE = 10000.0

kernel_name = "yoco_s5_mla_macaron_trunk"


def rms_norm(x, g):
    xf = x.astype(jnp.float32)
    y = xf * lax.rsqrt(jnp.mean(xf * xf, axis=-1, keepdims=True) + EPS)
    return (y * g.astype(jnp.float32)).astype(x.dtype)


def swiglu(h, w_in, w_out):
    gu = h @ w_in
    gate, up = gu[..., :D_FF], gu[..., D_FF:]
    return (jax.nn.silu(gate) * up) @ w_out


def rope_tables(positions):
    inv_freq = 1.0 / (ROPE_BASE ** (jnp.arange(0, ROPE_DIM, 2, dtype=jnp.float32) / ROPE_DIM))
    ang = positions.astype(jnp.float32)[..., None] * inv_freq
    return jnp.cos(ang), jnp.sin(ang)


def apply_rope(x, cos, sin):
    half = ROPE_DIM // 2
    xf = x.astype(jnp.float32)
    x1, x2 = xf[..., :half], xf[..., half:]
    return jnp.concatenate([x1 * cos - x2 * sin, x1 * sin + x2 * cos], axis=-1).astype(x.dtype)


def s5_mixer(h, lam_re, lam_im, log_step, b_re, b_im, c_re, c_im, d_skip, w_glu):
    bsz, seq, _ = h.shape
    f32 = jnp.float32
    u = h.astype(f32).reshape(bsz, seq, S5_GROUPS, S5_GROUP)
    step = jnp.exp(log_step.astype(f32))[:, None]
    lam = lax.complex(lam_re.astype(f32), lam_im.astype(f32))
    lam_bar = jnp.exp(lam * step)
    b = lax.complex(b_re.astype(f32), b_im.astype(f32))
    b_bar = ((lam_bar - 1.0) / lam)[..., None] * b
    bu = lax.complex(jnp.einsum('gpc,bsgc->bsgp', jnp.real(b_bar), u),
                     jnp.einsum('gpc,bsgc->bsgp', jnp.imag(b_bar), u))
    a = jnp.broadcast_to(lam_bar[None, None], (1, seq, S5_GROUPS, S5_STATE))

    def combine(e1, e2):
        a1, b1 = e1
        a2, b2 = e2
        return a1 * a2, a2 * b1 + b2

    _, states = lax.associative_scan(combine, (a, bu), axis=1)
    y = (jnp.einsum('gcp,bsgp->bsgc', c_re.astype(f32), jnp.real(states))
         - jnp.einsum('gcp,bsgp->bsgc', c_im.astype(f32), jnp.imag(states)))
    y = y.reshape(bsz, seq, D_MODEL) + d_skip.astype(f32) * h.astype(f32)
    y = jax.nn.gelu(y).astype(h.dtype)
    z = y @ w_glu
    return z[..., :D_MODEL] * jax.nn.sigmoid(z[..., D_MODEL:])


def shared_kv(h, cos, sin, kv_in_norm, w_dkv, kv_latent_norm, w_ukv):
    bsz, seq, _ = h.shape
    kv_a = rms_norm(h, kv_in_norm) @ w_dkv
    c_kv = rms_norm(kv_a[..., :KV_LORA], kv_latent_norm)
    k_rope = apply_rope(kv_a[..., KV_LORA:], cos, sin)
    kv = (c_kv @ w_ukv).reshape(bsz, seq, MLA_HEADS, NOPE_DIM + V_DIM)
    k_nope, v = kv[..., :NOPE_DIM], kv[..., NOPE_DIM:]
    k_rope_h = jnp.broadcast_to(k_rope[:, :, None, :], (bsz, seq, MLA_HEADS, ROPE_DIM))
    k = jnp.concatenate([k_nope, k_rope_h], axis=-1)
    return k, v


def mla_attention(h, cos, sin, k, v, w_dq, q_norm, w_uq, w_o):
    bsz, seq, _ = h.shape
    qk_dim = NOPE_DIM + ROPE_DIM
    c_q = rms_norm(h @ w_dq, q_norm)
    q = (c_q @ w_uq).reshape(bsz, seq, MLA_HEADS, qk_dim)
    q = jnp.concatenate([q[..., :NOPE_DIM],
                         apply_rope(q[..., NOPE_DIM:], cos[:, :, None, :], sin[:, :, None, :])], axis=-1)
    scale = 1.0 / math.sqrt(qk_dim)
    n_blk = seq // Q_BLOCK
    q_blocks = q.reshape(bsz, n_blk, Q_BLOCK, MLA_HEADS, qk_dim).transpose(1, 0, 2, 3, 4)
    q_chunk = (jnp.arange(seq, dtype=jnp.int32) // CHUNK).reshape(n_blk, Q_BLOCK)
    k_chunk = jnp.arange(seq, dtype=jnp.int32) // CHUNK
    neg = jnp.finfo(jnp.float32).min

    def attend(args):
        qb, qc = args
        s = jnp.einsum('bqhd,bkhd->bhqk', qb, k).astype(jnp.float32) * scale
        mask = k_chunk[None, :] <= qc[:, None]
        s = jnp.where(mask[None, None], s, neg)
        p = jax.nn.softmax(s, axis=-1).astype(v.dtype)
        return jnp.einsum('bhqk,bkhd->bqhd', p, v)

    o = lax.map(attend, (q_blocks, q_chunk))
    o = o.transpose(1, 0, 2, 3, 4).reshape(bsz, seq, MLA_HEADS * V_DIM)
    return o @ w_o


def setup_inputs(seed: int = 0) -> dict:
    key = jax.random.key(seed)
    ks = jax.random.split(key, 24)
    f32 = jnp.float32

    def nrm(k, shape, fan_in):
        return jax.random.normal(k, shape, f32) * (fan_in ** -0.5)

    def gain(k, shape):
        return 1.0 + 0.02 * jax.random.normal(k, shape, f32)

    x = jax.random.normal(ks[0], (BATCH, SEQ, D_MODEL), f32)
    start = jax.random.randint(ks[1], (BATCH, 1), 0, 4096, dtype=jnp.int32)
    positions = (start + jnp.arange(SEQ, dtype=jnp.int32)[None, :]).astype(jnp.int32)
    norm_gains = gain(ks[2], (DEPTH, 3, D_MODEL))
    ffn_w_in = nrm(ks[3], (DEPTH, 2, D_MODEL, 2 * D_FF), D_MODEL)
    ffn_w_out = nrm(ks[4], (DEPTH, 2, D_FF, D_MODEL), D_FF)
    n_idx = jnp.arange(S5_STATE, dtype=f32)
    s5_lambda_re = -0.5 * jnp.exp(0.05 * jax.random.normal(ks[5], (N_A, S5_GROUPS, S5_STATE), f32))
    s5_lambda_im = math.pi * n_idx + 0.05 * jax.random.normal(ks[6], (N_A, S5_GROUPS, S5_STATE), f32)
    s5_log_step = jax.random.uniform(ks[7], (N_A, S5_GROUPS), f32,
                                     math.log(S5_DT_MIN), math.log(S5_DT_MAX))
    s5_b_re = nrm(ks[8], (N_A, S5_GROUPS, S5_STATE, S5_GROUP), 2 * S5_GROUP)
    s5_b_im = nrm(ks[9], (N_A, S5_GROUPS, S5_STATE, S5_GROUP), 2 * S5_GROUP)
    s5_c_re = nrm(ks[10], (N_A, S5_GROUPS, S5_GROUP, S5_STATE), S5_STATE)
    s5_c_im = nrm(ks[11], (N_A, S5_GROUPS, S5_GROUP, S5_STATE), S5_STATE)
    s5_d = jax.random.normal(ks[12], (N_A, D_MODEL), f32)
    s5_w_glu = nrm(ks[13], (N_A, D_MODEL, 2 * D_MODEL), D_MODEL)
    kv_in_norm = gain(ks[14], (D_MODEL,))
    w_dkv = nrm(ks[15], (D_MODEL, KV_LORA + ROPE_DIM), D_MODEL)
    kv_latent_norm = gain(ks[16], (KV_LORA,))
    w_ukv = nrm(ks[17], (KV_LORA, MLA_HEADS * (NOPE_DIM + V_DIM)), KV_LORA)
    mla_w_dq = nrm(ks[18], (N_B, D_MODEL, Q_LORA), D_MODEL)
    mla_q_norm = gain(ks[19], (N_B, Q_LORA))
    mla_w_uq = nrm(ks[20], (N_B, Q_LORA, MLA_HEADS * (NOPE_DIM + ROPE_DIM)), Q_LORA)
    mla_w_o = nrm(ks[21], (N_B, MLA_HEADS * V_DIM, D_MODEL), MLA_HEADS * V_DIM)
    final_norm = gain(ks[22], (D_MODEL,))
    return {"x": x, "positions": positions, "norm_gains": norm_gains,
            "ffn_w_in": ffn_w_in, "ffn_w_out": ffn_w_out,
            "s5_lambda_re": s5_lambda_re, "s5_lambda_im": s5_lambda_im,
            "s5_log_step": s5_log_step, "s5_b_re": s5_b_re, "s5_b_im": s5_b_im,
            "s5_c_re": s5_c_re, "s5_c_im": s5_c_im, "s5_d": s5_d, "s5_w_glu": s5_w_glu,
            "kv_in_norm": kv_in_norm, "w_dkv": w_dkv, "kv_latent_norm": kv_latent_norm,
            "w_ukv": w_ukv, "mla_w_dq": mla_w_dq, "mla_q_norm": mla_q_norm,
            "mla_w_uq": mla_w_uq, "mla_w_o": mla_w_o, "final_norm": final_norm}


def reference(x, positions, norm_gains, ffn_w_in, ffn_w_out,
              s5_lambda_re, s5_lambda_im, s5_log_step, s5_b_re, s5_b_im,
              s5_c_re, s5_c_im, s5_d, s5_w_glu,
              kv_in_norm, w_dkv, kv_latent_norm, w_ukv,
              mla_w_dq, mla_q_norm, mla_w_uq, mla_w_o, final_norm):
    cos, sin = rope_tables(positions)
    k_shared = None
    v_shared = None
    for l in range(DEPTH):
        g = norm_gains[l]
        x = x + 0.5 * swiglu(rms_norm(x, g[0]), ffn_w_in[l, 0], ffn_w_out[l, 0])
        h = rms_norm(x, g[1])
        if l < N_A:
            x = x + s5_mixer(h, s5_lambda_re[l], s5_lambda_im[l], s5_log_step[l],
                             s5_b_re[l], s5_b_im[l], s5_c_re[l], s5_c_im[l],
                             s5_d[l], s5_w_glu[l])
        else:
            j = l - N_A
            x = x + mla_attention(h, cos, sin, k_shared, v_shared,
                                  mla_w_dq[j], mla_q_norm[j], mla_w_uq[j], mla_w_o[j])
        x = x + 0.5 * swiglu(rms_norm(x, g[2]), ffn_w_in[l, 1], ffn_w_out[l, 1])
        if l == N_A - 1 and N_B > 0:
            k_shared, v_shared = shared_kv(x, cos, sin, kv_in_norm, w_dkv,
                                           kv_latent_norm, w_ukv)
    return rms_norm(x, final_norm)
```

```python
import functools
import math

import jax
import jax.numpy as jnp
from jax import lax
from jax.experimental import pallas as pl
from jax.experimental.pallas import tpu as pltpu

D_MODEL = 1024
BATCH = 8
SEQ = 2048
CHUNK = 64
EPS = 1e-6
D_FF = 2816
S5_GROUP = 16
S5_GROUPS = D_MODEL // S5_GROUP
S5_STATE = 64
MLA_HEADS = 8
NOPE_DIM = 128
ROPE_DIM = 64
V_DIM = 128
Q_LORA = 256
KV_LORA = 128
ROPE_BASE = 10000.0

LANES = 128
SUBLANES = 8
MXU_DIM = 256
VMEM_LIMIT_BYTES = 56 * 1024 * 1024

HEAD_PAD = 2 * LANES
FFN_ROWS = 512
FFN_CHUNKS = ((0, 1024), (1024, 2048), (2048, 2816))
S5_STEPS = 64
S5_ROWS = S5_STEPS * BATCH
S5_KT = D_MODEL // MXU_DIM
S5_GPT = MXU_DIM // S5_GROUP
S5_MODES = S5_GPT * S5_STATE
S5_HALF = S5_MODES // 2
PROJ_ROWS = 512
ATT_Q = 256

BF16 = jnp.bfloat16
F32 = jnp.float32


def _rms(x, gain):
    return x * lax.rsqrt(jnp.mean(x * x, axis=-1, keepdims=True) + EPS) * gain


def _resident(shape):
    zeros = (0,) * len(shape)
    return pl.BlockSpec(shape, lambda *_: zeros, pipeline_mode=pl.Buffered(1))


def _params(n_grid_dims):
    return pltpu.CompilerParams(
        dimension_semantics=("arbitrary",) * n_grid_dims,
        vmem_limit_bytes=VMEM_LIMIT_BYTES)


def _ffn_kernel(*refs, with_attn, with_final):
    refs = list(refs)
    x_ref = refs.pop(0)
    if with_attn:
        o_ref, wo_ref = refs.pop(0), refs.pop(0)
    g_ref, win_ref, wout_ref = refs.pop(0), refs.pop(0), refs.pop(0)
    if with_final:
        gf_ref = refs.pop(0)
    out_ref = refs.pop(0)

    x = x_ref[...]
    if with_attn:
        x = x + jnp.dot(o_ref[...], wo_ref[...], preferred_element_type=F32)
    h = _rms(x, g_ref[...]).astype(BF16)
    acc = None
    for c0, c1 in FFN_CHUNKS:
        gate = jnp.dot(h, win_ref[:, c0:c1], preferred_element_type=F32)
        up = jnp.dot(h, win_ref[:, D_FF + c0:D_FF + c1], preferred_element_type=F32)
        a = (jax.nn.silu(gate) * up).astype(BF16)
        part = jnp.dot(a, wout_ref[c0:c1, :], preferred_element_type=F32)
        acc = part if acc is None else acc + part
    y = x + 0.5 * acc
    if with_final:
        y = _rms(y, gf_ref[...])
    out_ref[...] = y


def _ffn(x, gain, w_in, w_out, attn=None, final_gain=None):
    rows = x.shape[0]
    row_spec = pl.BlockSpec((FFN_ROWS, D_MODEL), lambda i: (i, 0))
    args, specs = [x], [row_spec]
    if attn is not None:
        o, w_o = attn
        args += [o, w_o]
        specs += [row_spec, _resident(w_o.shape)]
    args += [gain, w_in, w_out]
    specs += [_resident(gain.shape), _resident(w_in.shape), _resident(w_out.shape)]
    if final_gain is not None:
        args.append(final_gain)
        specs.append(_resident(final_gain.shape))
    return pl.pallas_call(
        functools.partial(_ffn_kernel, with_attn=attn is not None,
                          with_final=final_gain is not None),
        grid=(rows // FFN_ROWS,),
        in_specs=specs,
        out_specs=row_spec,
        out_shape=jax.ShapeDtypeStruct((rows, D_MODEL), F32),
        compiler_params=_params(1),
        name="ffn",
    )(*args)


def _s5_kernel(x_ref, g_ref, wb_ref, lam_ref, wc_ref, d_ref, wglu_ref, out_ref,
               bu_ref, state_ref):
    @pl.when(pl.program_id(0) == 0)
    def _():
        state_ref[...] = jnp.zeros_like(state_ref)

    x = x_ref[...]
    h = _rms(x, g_ref[...])
    hb = h.astype(BF16)
    ys = []
    for kt in range(S5_KT):
        bu_ref[...] = jnp.dot(hb[:, kt * MXU_DIM:(kt + 1) * MXU_DIM], wb_ref[kt],
                              preferred_element_type=F32)
        for half in range(2):
            re0 = half * S5_HALF
            im0 = S5_MODES + half * S5_HALF
            lam_re = lam_ref[kt, :, re0:re0 + S5_HALF]
            lam_im = lam_ref[kt, :, im0:im0 + S5_HALF]

            def step(t, carry, re0=re0, im0=im0, lam_re=lam_re, lam_im=lam_im):
                s_re, s_im = carry
                r0 = pl.multiple_of(t * SUBLANES, SUBLANES)
                n_re = lam_re * s_re - lam_im * s_im + bu_ref[pl.ds(r0, SUBLANES), re0:re0 + S5_HALF]
                n_im = lam_re * s_im + lam_im * s_re + bu_ref[pl.ds(r0, SUBLANES), im0:im0 + S5_HALF]
                bu_ref[pl.ds(r0, SUBLANES), re0:re0 + S5_HALF] = n_re
                bu_ref[pl.ds(r0, SUBLANES), im0:im0 + S5_HALF] = n_im
                return n_re, n_im

            s_re, s_im = lax.fori_loop(
                0, S5_STEPS, step,
                (state_ref[kt, :, re0:re0 + S5_HALF], state_ref[kt, :, im0:im0 + S5_HALF]),
                unroll=8)
            state_ref[kt, :, re0:re0 + S5_HALF] = s_re
            state_ref[kt, :, im0:im0 + S5_HALF] = s_im
        ys.append(jnp.dot(bu_ref[...].astype(BF16), wc_ref[kt], preferred_element_type=F32))
    y = jnp.concatenate(ys, axis=1) + d_ref[...] * h
    y = jax.nn.gelu(y, approximate=True).astype(BF16)
    z = jnp.dot(y, wglu_ref[...], preferred_element_type=F32)
    out_ref[...] = x + z[:, :D_MODEL] * jax.nn.sigmoid(z[:, D_MODEL:])


def _s5(x, gain, wb, lam, wc, d_skip, w_glu):
    rows = x.shape[0]
    row_spec = pl.BlockSpec((S5_ROWS, D_MODEL), lambda i: (i, 0))
    return pl.pallas_call(
        _s5_kernel,
        grid=(rows // S5_ROWS,),
        in_specs=[row_spec, _resident(gain.shape), _resident(wb.shape), _resident(lam.shape),
                  _resident(wc.shape), _resident(d_skip.shape), _resident(w_glu.shape)],
        out_specs=row_spec,
        out_shape=jax.ShapeDtypeStruct((rows, D_MODEL), F32),
        scratch_shapes=[pltpu.VMEM((S5_ROWS, 2 * S5_MODES), F32),
                        pltpu.VMEM((S5_KT, SUBLANES, 2 * S5_MODES), F32)],
        compiler_params=_params(1),
        name="s5_mixer",
    )(x, gain, wb, lam, wc, d_skip, w_glu)


def _s5_weights(lam_re, lam_im, log_step, b_re, b_im, c_re, c_im):
    step = jnp.exp(log_step.astype(F32))[:, None]
    lam = lax.complex(lam_re.astype(F32), lam_im.astype(F32))
    lam_bar = jnp.exp(lam * step)
    b = lax.complex(b_re.astype(F32), b_im.astype(F32))
    b_bar = ((lam_bar - 1.0) / lam)[..., None] * b
    eye = jnp.eye(S5_GPT, dtype=F32)

    def pack_b(part):
        part = part.reshape(S5_KT, S5_GPT, S5_STATE, S5_GROUP)
        return jnp.einsum('kgpc,gh->kgchp', part, eye).reshape(S5_KT, MXU_DIM, S5_MODES)

    def pack_c(part):
        part = part.reshape(S5_KT, S5_GPT, S5_GROUP, S5_STATE)
        return jnp.einsum('kgcp,gh->kgphc', part, eye).reshape(S5_KT, S5_MODES, MXU_DIM)

    wb = jnp.concatenate([pack_b(jnp.real(b_bar)), pack_b(jnp.imag(b_bar))], axis=-1)
    wc = jnp.concatenate([pack_c(c_re.astype(F32)), -pack_c(c_im.astype(F32))], axis=1)
    lam_row = jnp.concatenate([jnp.real(lam_bar).reshape(S5_KT, S5_MODES),
                               jnp.imag(lam_bar).reshape(S5_KT, S5_MODES)], axis=-1)
    lam_tile = jnp.broadcast_to(lam_row[:, None, :], (S5_KT, SUBLANES, 2 * S5_MODES))
    return wb.astype(BF16), lam_tile, wc.astype(BF16)


def _kv_kernel(x_ref, cos_ref, sin_ref, gin_ref, wd_ref, glat_ref, wuk_ref, wuv_ref,
               k_ref, v_ref):
    h = _rms(x_ref[...], gin_ref[...]).astype(BF16)
    kv_a = jnp.dot(h, wd_ref[...], preferred_element_type=F32)
    c_kv = _rms(kv_a[:, :KV_LORA], glat_ref[...]).astype(BF16)
    k_rope = (kv_a[:, LANES:2 * LANES] * cos_ref[...]
              + kv_a[:, 2 * LANES:3 * LANES] * sin_ref[...]).astype(BF16)
    k_nope = jnp.dot(c_kv, wuk_ref[...], preferred_element_type=F32).astype(BF16)
    v_ref[...] = jnp.dot(c_kv, wuv_ref[...], preferred_element_type=F32).astype(BF16)
    for hd in range(MLA_HEADS):
        k_ref[:, hd * HEAD_PAD:hd * HEAD_PAD + NOPE_DIM] = k_nope[:, hd * NOPE_DIM:(hd + 1) * NOPE_DIM]
        k_ref[:, hd * HEAD_PAD + NOPE_DIM:(hd + 1) * HEAD_PAD] = k_rope


def _kv(x, cos_p, sin_p, gin, wd, glat, wuk, wuv):
    rows = x.shape[0]

    def rows_of(width):
        return pl.BlockSpec((PROJ_ROWS, width), lambda i: (i, 0))

    return pl.pallas_call(
        _kv_kernel,
        grid=(rows // PROJ_ROWS,),
        in_specs=[rows_of(D_MODEL), rows_of(LANES), rows_of(LANES), _resident(gin.shape),
                  _resident(wd.shape), _resident(glat.shape), _resident(wuk.shape),
                  _resident(wuv.shape)],
        out_specs=[rows_of(MLA_HEADS * HEAD_PAD), rows_of(MLA_HEADS * V_DIM)],
        out_shape=[jax.ShapeDtypeStruct((rows, MLA_HEADS * HEAD_PAD), BF16),
                   jax.ShapeDtypeStruct((rows, MLA_HEADS * V_DIM), BF16)],
        compiler_params=_params(1),
        name="mla_kv",
    )(x, cos_p, sin_p, gin, wd, glat, wuk, wuv)


def _q_kernel(x_ref, cos_ref, sin_ref, g_ref, wdq_ref, gq_ref, wq_ref, wrot_ref, q_ref):
    scale = 1.0 / math.sqrt(NOPE_DIM + ROPE_DIM)
    h = _rms(x_ref[...], g_ref[...]).astype(BF16)
    c_q = _rms(jnp.dot(h, wdq_ref[...], preferred_element_type=F32), gq_ref[...]).astype(BF16)
    q_main = jnp.dot(c_q, wq_ref[...], preferred_element_type=F32)
    q_rot = jnp.dot(c_q, wrot_ref[...], preferred_element_type=F32)
    cos_p, sin_p = cos_ref[...], sin_ref[...]
    for hd in range(MLA_HEADS):
        base = hd * HEAD_PAD
        q_ref[:, base:base + NOPE_DIM] = (q_main[:, base:base + NOPE_DIM] * scale).astype(BF16)
        rope = (q_main[:, base + NOPE_DIM:base + HEAD_PAD] * cos_p
                + q_rot[:, hd * LANES:(hd + 1) * LANES] * sin_p)
        q_ref[:, base + NOPE_DIM:base + HEAD_PAD] = (rope * scale).astype(BF16)


def _q(x, cos_p, sin_p, gain, wdq, gq, wq, wrot):
    rows = x.shape[0]

    def rows_of(width):
        return pl.BlockSpec((PROJ_ROWS, width), lambda i: (i, 0))

    return pl.pallas_call(
        _q_kernel,
        grid=(rows // PROJ_ROWS,),
        in_specs=[rows_of(D_MODEL), rows_of(LANES), rows_of(LANES), _resident(gain.shape),
                  _resident(wdq.shape), _resident(gq.shape), _resident(wq.shape),
                  _resident(wrot.shape)],
        out_specs=rows_of(MLA_HEADS * HEAD_PAD),
        out_shape=jax.ShapeDtypeStruct((rows, MLA_HEADS * HEAD_PAD), BF16),
        compiler_params=_params(1),
        name="mla_q",
    )(x, cos_p, sin_p, gain, wdq, gq, wq, wrot)


def _attn_kernel(q_ref, k_ref, v_ref, o_ref):
    row_chunk = lax.broadcasted_iota(jnp.int32, (ATT_Q, ATT_Q), 0) // CHUNK
    col_chunk = lax.broadcasted_iota(jnp.int32, (ATT_Q, ATT_Q), 1) // CHUNK
    diag_mask = col_chunk <= row_chunk
    neg = jnp.finfo(F32).min
    for i in range(SEQ // ATT_Q):
        q0, n_keys = i * ATT_Q, (i + 1) * ATT_Q
        s = lax.dot_general(q_ref[q0:q0 + ATT_Q, :], k_ref[0:n_keys, :],
                            (((1,), (1,)), ((), ())), preferred_element_type=F32)
        s_diag = jnp.where(diag_mask, s[:, q0:n_keys], neg)
        s = s_diag if i == 0 else jnp.concatenate([s[:, :q0], s_diag], axis=1)
        m = jnp.max(s, axis=-1, keepdims=True)
        p = jnp.exp(s - m)
        denom = jnp.sum(p, axis=-1, keepdims=True)
        o = jnp.dot(p.astype(BF16), v_ref[0:n_keys, :], preferred_element_type=F32)
        o_ref[q0:q0 + ATT_Q, :] = (o / denom).astype(BF16)


def _attention(q, k, v):
    return pl.pallas_call(
        _attn_kernel,
        grid=(BATCH, MLA_HEADS),
        in_specs=[pl.BlockSpec((None, SEQ, HEAD_PAD), lambda b, h: (b, 0, h)),
                  pl.BlockSpec((None, SEQ, HEAD_PAD), lambda b, h: (b, 0, h)),
                  pl.BlockSpec((None, SEQ, V_DIM), lambda b, h: (b, 0, h))],
        out_specs=pl.BlockSpec((None, SEQ, V_DIM), lambda b, h: (b, 0, h)),
        out_shape=jax.ShapeDtypeStruct((BATCH, SEQ, MLA_HEADS * V_DIM), BF16),
        compiler_params=_params(2),
        name="mla_attention",
    )(q, k, v)


def _rope_tables(positions):
    inv_freq = 1.0 / (ROPE_BASE ** (jnp.arange(0, ROPE_DIM, 2, dtype=F32) / ROPE_DIM))
    ang = positions.astype(F32)[..., None] * inv_freq
    cos, sin = jnp.cos(ang), jnp.sin(ang)
    zeros = jnp.zeros(cos.shape[:-1] + (LANES - ROPE_DIM,), F32)
    cos_p = jnp.concatenate([cos, cos, zeros], axis=-1)
    sin_p = jnp.concatenate([-sin, sin, zeros], axis=-1)
    return cos_p.reshape(-1, LANES), sin_p.reshape(-1, LANES)


def _rot_cols(w):
    half = ROPE_DIM // 2
    return jnp.concatenate([w[..., half:], w[..., :half]], axis=-1)


def _pad_cols(w, width):
    return jnp.pad(w, [(0, 0)] * (w.ndim - 1) + [(0, width - w.shape[-1])])


def kernel(x, positions, norm_gains, ffn_w_in, ffn_w_out, s5_lambda_re, s5_lambda_im, s5_log_step, s5_b_re, s5_b_im, s5_c_re, s5_c_im, s5_d, s5_w_glu, kv_in_norm, w_dkv, kv_latent_norm, w_ukv, mla_w_dq, mla_q_norm, mla_w_uq, mla_w_o, final_norm):
    assert x.shape == (BATCH, SEQ, D_MODEL) and norm_gains.shape[0] == 2
    rows = BATCH * SEQ
    gains = norm_gains.astype(F32).reshape(2, 3, 1, D_MODEL)
    w_in = ffn_w_in.astype(BF16)
    w_out = ffn_w_out.astype(BF16)
    cos_p, sin_p = _rope_tables(positions)

    xt = x.astype(F32).transpose(1, 0, 2).reshape(rows, D_MODEL)
    xt = _ffn(xt, gains[0, 0], w_in[0, 0], w_out[0, 0])
    wb, lam_tile, wc = _s5_weights(s5_lambda_re[0], s5_lambda_im[0], s5_log_step[0],
                                   s5_b_re[0], s5_b_im[0], s5_c_re[0], s5_c_im[0])
    xt = _s5(xt, gains[0, 1], wb, lam_tile, wc, s5_d[0].astype(F32).reshape(1, D_MODEL),
             s5_w_glu[0].astype(BF16))
    xt = _ffn(xt, gains[0, 2], w_in[0, 1], w_out[0, 1])
    xb = xt.reshape(SEQ, BATCH, D_MODEL).transpose(1, 0, 2).reshape(rows, D_MODEL)

    w_rope = w_dkv[:, KV_LORA:]
    wd = jnp.concatenate([w_dkv[:, :KV_LORA], _pad_cols(w_rope, LANES),
                          _pad_cols(_rot_cols(w_rope), LANES)], axis=-1).astype(BF16)
    w_ukv_h = w_ukv.reshape(KV_LORA, MLA_HEADS, NOPE_DIM + V_DIM)
    wuk = w_ukv_h[:, :, :NOPE_DIM].reshape(KV_LORA, MLA_HEADS * NOPE_DIM).astype(BF16)
    wuv = w_ukv_h[:, :, NOPE_DIM:].reshape(KV_LORA, MLA_HEADS * V_DIM).astype(BF16)
    k, v = _kv(xb, cos_p, sin_p, kv_in_norm.astype(F32).reshape(1, D_MODEL), wd,
               kv_latent_norm.astype(F32).reshape(1, KV_LORA), wuk, wuv)

    xb = _ffn(xb, gains[1, 0], w_in[1, 0], w_out[1, 0])
    w_uq_h = mla_w_uq[0].reshape(Q_LORA, MLA_HEADS, NOPE_DIM + ROPE_DIM)
    wq = _pad_cols(w_uq_h, HEAD_PAD).reshape(Q_LORA, MLA_HEADS * HEAD_PAD).astype(BF16)
    wrot = _pad_cols(_rot_cols(w_uq_h[:, :, NOPE_DIM:]), LANES)
    wrot = wrot.reshape(Q_LORA, MLA_HEADS * LANES).astype(BF16)
    q = _q(xb, cos_p, sin_p, gains[1, 1], mla_w_dq[0].astype(BF16),
           mla_q_norm[0].astype(F32).reshape(1, Q_LORA), wq, wrot)
    o = _attention(q.reshape(BATCH, SEQ, -1), k.reshape(BATCH, SEQ, -1),
                   v.reshape(BATCH, SEQ, -1))
    out = _ffn(xb, gains[1, 2], w_in[1, 1], w_out[1, 1],
               attn=(o.reshape(rows, MLA_HEADS * V_DIM), mla_w_o[0].astype(BF16)),
               final_gain=final_norm.astype(F32).reshape(1, D_MODEL))
    return out.reshape(BATCH, SEQ, D_MODEL).astype(x.dtype)
```

```python
import functools
import math

import jax
import jax.numpy as jnp
from jax import lax
from jax.experimental import pallas as pl
from jax.experimental.pallas import tpu as pltpu

D_MODEL = 1024
BATCH = 8
SEQ = 2048
CHUNK = 64
EPS = 1e-6
D_FF = 2816
S5_GROUP = 16
S5_GROUPS = D_MODEL // S5_GROUP
S5_STATE = 64
MLA_HEADS = 8
NOPE_DIM = 128
ROPE_DIM = 64
V_DIM = 128
Q_LORA = 256
KV_LORA = 128
ROPE_BASE = 10000.0

LANES = 128
SUBLANES = 8
MXU_DIM = 256
VMEM_LIMIT_BYTES = 56 * 1024 * 1024

HEAD_PAD = 2 * LANES
FFN_STEPS = 64
FFN_ROWS = FFN_STEPS * BATCH
FFN_CHUNKS = ((0, 1024), (1024, 2048), (2048, 2816))
S5_STEPS = 64
S5_ROWS = S5_STEPS * BATCH
S5_KT = D_MODEL // MXU_DIM
S5_GPT = MXU_DIM // S5_GROUP
S5_MODES = S5_GPT * S5_STATE
S5_HALF = S5_MODES // 2
PROJ_ROWS = 512
ATT_Q = 256

BF16 = jnp.bfloat16
F32 = jnp.float32


def _rms(x, gain):
    return x * lax.rsqrt(jnp.mean(x * x, axis=-1, keepdims=True) + EPS) * gain


def _resident(shape):
    zeros = (0,) * len(shape)
    return pl.BlockSpec(shape, lambda *_: zeros, pipeline_mode=pl.Buffered(1))


def _params(n_grid_dims):
    return pltpu.CompilerParams(
        dimension_semantics=("arbitrary",) * n_grid_dims,
        vmem_limit_bytes=VMEM_LIMIT_BYTES)


def _ffn_kernel(*refs, relayout, with_attn, with_final):
    refs = list(refs)
    x_ref = refs.pop(0)
    if with_attn:
        o_ref, wo_ref = refs.pop(0), refs.pop(0)
    g_ref, win_ref, wout_ref = refs.pop(0), refs.pop(0), refs.pop(0)
    if with_final:
        gf_ref = refs.pop(0)
    out_ref = refs.pop(0)

    x = x_ref[...]
    if relayout == "bt_to_tb":
        x = jnp.swapaxes(x, 0, 1).reshape(FFN_ROWS, D_MODEL)
    elif relayout == "tb_to_bt":
        x = jnp.swapaxes(x.reshape(FFN_STEPS, BATCH, D_MODEL), 0, 1).reshape(FFN_ROWS, D_MODEL)
    if with_attn:
        x = x + jnp.dot(o_ref[...], wo_ref[...], preferred_element_type=F32)
    h = _rms(x, g_ref[...]).astype(BF16)
    acc = None
    for c0, c1 in FFN_CHUNKS:
        gate = jnp.dot(h, win_ref[:, c0:c1], preferred_element_type=F32)
        up = jnp.dot(h, win_ref[:, D_FF + c0:D_FF + c1], preferred_element_type=F32)
        a = (jax.nn.silu(gate) * up).astype(BF16)
        part = jnp.dot(a, wout_ref[c0:c1, :], preferred_element_type=F32)
        acc = part if acc is None else acc + part
    y = x + 0.5 * acc
    if with_final:
        y = _rms(y, gf_ref[...])
    out_ref[...] = y.reshape(out_ref.shape)


def _ffn(x, gain, w_in, w_out, layer, half, relayout=None, attn=None, final_gain=None):
    rows = BATCH * SEQ
    row_spec = pl.BlockSpec((FFN_ROWS, D_MODEL), lambda i: (i, 0))
    cube_spec = pl.BlockSpec((BATCH, FFN_STEPS, D_MODEL), lambda i: (0, i, 0))
    flat = jax.ShapeDtypeStruct((rows, D_MODEL), F32)
    cube = jax.ShapeDtypeStruct((BATCH, SEQ, D_MODEL), F32)
    in_spec = cube_spec if relayout == "bt_to_tb" else row_spec
    out_spec, out_shape = (cube_spec, cube) if relayout == "tb_to_bt" else (row_spec, flat)

    def weight(w):
        return pl.BlockSpec((None, None) + w.shape[2:], lambda i: (layer, half, 0, 0),
                            pipeline_mode=pl.Buffered(1))

    args, specs = [x], [in_spec]
    if attn is not None:
        o, w_o = attn
        args += [o, w_o]
        specs += [row_spec, _resident(w_o.shape)]
    args += [gain, w_in, w_out]
    specs += [_resident(gain.shape), weight(w_in), weight(w_out)]
    if final_gain is not None:
        args.append(final_gain)
        specs.append(_resident(final_gain.shape))
    return pl.pallas_call(
        functools.partial(_ffn_kernel, relayout=relayout, with_attn=attn is not None,
                          with_final=final_gain is not None),
        grid=(rows // FFN_ROWS,),
        in_specs=specs,
        out_specs=out_spec,
        out_shape=out_shape,
        compiler_params=_params(1),
        name="ffn",
    )(*args)


def _s5_kernel(x_ref, g_ref, wb_ref, lam_ref, wc_ref, d_ref, wglu_ref, out_ref,
               bu_ref, state_ref):
    @pl.when(pl.program_id(0) == 0)
    def _():
        state_ref[...] = jnp.zeros_like(state_ref)

    x = x_ref[...]
    h = _rms(x, g_ref[...])
    hb = h.astype(BF16)
    ys = []
    for kt in range(S5_KT):
        bu_ref[...] = jnp.dot(hb[:, kt * MXU_DIM:(kt + 1) * MXU_DIM], wb_ref[kt],
                              preferred_element_type=F32)
        for half in range(2):
            re0 = half * S5_HALF
            im0 = S5_MODES + half * S5_HALF
            lam_re = lam_ref[kt, :, re0:re0 + S5_HALF]
            lam_im = lam_ref[kt, :, im0:im0 + S5_HALF]

            def step(t, carry, re0=re0, im0=im0, lam_re=lam_re, lam_im=lam_im):
                s_re, s_im = carry
                r0 = pl.multiple_of(t * SUBLANES, SUBLANES)
                n_re = lam_re * s_re - lam_im * s_im + bu_ref[pl.ds(r0, SUBLANES), re0:re0 + S5_HALF]
                n_im = lam_re * s_im + lam_im * s_re + bu_ref[pl.ds(r0, SUBLANES), im0:im0 + S5_HALF]
                bu_ref[pl.ds(r0, SUBLANES), re0:re0 + S5_HALF] = n_re
                bu_ref[pl.ds(r0, SUBLANES), im0:im0 + S5_HALF] = n_im
                return n_re, n_im

            s_re, s_im = lax.fori_loop(
                0, S5_STEPS, step,
                (state_ref[kt, :, re0:re0 + S5_HALF], state_ref[kt, :, im0:im0 + S5_HALF]),
                unroll=8)
            state_ref[kt, :, re0:re0 + S5_HALF] = s_re
            state_ref[kt, :, im0:im0 + S5_HALF] = s_im
        ys.append(jnp.dot(bu_ref[...].astype(BF16), wc_ref[kt], preferred_element_type=F32))
    y = jnp.concatenate(ys, axis=1) + d_ref[...] * h
    y = jax.nn.gelu(y, approximate=True).astype(BF16)
    z = jnp.dot(y, wglu_ref[...], preferred_element_type=F32)
    out_ref[...] = x + z[:, :D_MODEL] * jax.nn.sigmoid(z[:, D_MODEL:])


def _s5(x, gain, wb, lam, wc, d_skip, w_glu):
    rows = x.shape[0]
    row_spec = pl.BlockSpec((S5_ROWS, D_MODEL), lambda i: (i, 0))
    return pl.pallas_call(
        _s5_kernel,
        grid=(rows // S5_ROWS,),
        in_specs=[row_spec, _resident(gain.shape), _resident(wb.shape), _resident(lam.shape),
                  _resident(wc.shape), _resident(d_skip.shape), _resident(w_glu.shape)],
        out_specs=row_spec,
        out_shape=jax.ShapeDtypeStruct((rows, D_MODEL), F32),
        scratch_shapes=[pltpu.VMEM((S5_ROWS, 2 * S5_MODES), F32),
                        pltpu.VMEM((S5_KT, SUBLANES, 2 * S5_MODES), F32)],
        compiler_params=_params(1),
        name="s5_mixer",
    )(x, gain, wb, lam, wc, d_skip, w_glu)


def _s5_disc_kernel(lam_re_ref, lam_im_ref, lam_re_rep_ref, lam_im_rep_ref, log_step_ref,
                    b_re_ref, b_im_ref, lbar_re_ref, lbar_im_ref, bbar_re_ref, bbar_im_ref):
    step = jnp.exp(log_step_ref[...])

    def lam_bar(lam_re, lam_im):
        mag = jnp.exp(lam_re * step)
        return mag * jnp.cos(lam_im * step), mag * jnp.sin(lam_im * step)

    lbar_re_ref[...], lbar_im_ref[...] = lam_bar(lam_re_ref[...], lam_im_ref[...])
    lam_re, lam_im = lam_re_rep_ref[...], lam_im_rep_ref[...]
    lb_re, lb_im = lam_bar(lam_re, lam_im)
    n_re, n_im = lb_re - 1.0, lb_im
    den = lam_re * lam_re + lam_im * lam_im
    f_re = (n_re * lam_re + n_im * lam_im) / den
    f_im = (n_im * lam_re - n_re * lam_im) / den
    b_re, b_im = b_re_ref[...], b_im_ref[...]
    bbar_re_ref[...] = f_re * b_re - f_im * b_im
    bbar_im_ref[...] = f_re * b_im + f_im * b_re


def _s5_weights(lam_re, lam_im, log_step, b_re, b_im, c_re, c_im):
    pc = S5_STATE * S5_GROUP
    gp = jax.ShapeDtypeStruct((S5_GROUPS, S5_STATE), F32)
    gpc = jax.ShapeDtypeStruct((S5_GROUPS, pc), F32)
    lbar_re, lbar_im, bbar_re, bbar_im = pl.pallas_call(
        _s5_disc_kernel, out_shape=[gp, gp, gpc, gpc], name="s5_discretise",
    )(lam_re.astype(F32), lam_im.astype(F32),
      jnp.repeat(lam_re.astype(F32), S5_GROUP, axis=1), jnp.repeat(lam_im.astype(F32), S5_GROUP, axis=1),
      log_step.astype(F32).reshape(S5_GROUPS, 1),
      b_re.astype(F32).reshape(S5_GROUPS, pc), b_im.astype(F32).reshape(S5_GROUPS, pc))
    eye = jnp.eye(S5_GPT, dtype=F32)

    def pack_b(part):
        part = part.reshape(S5_KT, S5_GPT, S5_STATE, S5_GROUP)
        return jnp.einsum('kgpc,gh->kgchp', part, eye).reshape(S5_KT, MXU_DIM, S5_MODES)

    def pack_c(part):
        part = part.reshape(S5_KT, S5_GPT, S5_GROUP, S5_STATE)
        return jnp.einsum('kgcp,gh->kgphc', part, eye).reshape(S5_KT, S5_MODES, MXU_DIM)

    wb = jnp.concatenate([pack_b(bbar_re), pack_b(bbar_im)], axis=-1)
    wc = jnp.concatenate([pack_c(c_re.astype(F32)), -pack_c(c_im.astype(F32))], axis=1)
    lam_row = jnp.concatenate([lbar_re.reshape(S5_KT, S5_MODES),
                               lbar_im.reshape(S5_KT, S5_MODES)], axis=-1)
    lam_tile = jnp.broadcast_to(lam_row[:, None, :], (S5_KT, SUBLANES, 2 * S5_MODES))
    return wb.astype(BF16), lam_tile, wc.astype(BF16)


def _kv_kernel(x_ref, cos_ref, sin_ref, gin_ref, wd_ref, glat_ref, wuk_ref, wuv_ref,
               k_ref, v_ref):
    h = _rms(x_ref[...], gin_ref[...]).astype(BF16)
    kv_a = jnp.dot(h, wd_ref[...], preferred_element_type=F32)
    c_kv = _rms(kv_a[:, :KV_LORA], glat_ref[...]).astype(BF16)
    k_rope = (kv_a[:, LANES:2 * LANES] * cos_ref[...]
              + kv_a[:, 2 * LANES:3 * LANES] * sin_ref[...]).astype(BF16)
    k_nope = jnp.dot(c_kv, wuk_ref[...], preferred_element_type=F32).astype(BF16)
    v_ref[...] = jnp.dot(c_kv, wuv_ref[...], preferred_element_type=F32).astype(BF16)
    for hd in range(MLA_HEADS):
        k_ref[:, hd * HEAD_PAD:hd * HEAD_PAD + NOPE_DIM] = k_nope[:, hd * NOPE_DIM:(hd + 1) * NOPE_DIM]
        k_ref[:, hd * HEAD_PAD + NOPE_DIM:(hd + 1) * HEAD_PAD] = k_rope


def _kv(x, cos_p, sin_p, gin, wd, glat, wuk, wuv):
    rows = x.shape[0]

    def rows_of(width):
        return pl.BlockSpec((PROJ_ROWS, width), lambda i: (i, 0))

    return pl.pallas_call(
        _kv_kernel,
        grid=(rows // PROJ_ROWS,),
        in_specs=[rows_of(D_MODEL), rows_of(LANES), rows_of(LANES), _resident(gin.shape),
                  _resident(wd.shape), _resident(glat.shape), _resident(wuk.shape),
                  _resident(wuv.shape)],
        out_specs=[rows_of(MLA_HEADS * HEAD_PAD), rows_of(MLA_HEADS * V_DIM)],
        out_shape=[jax.ShapeDtypeStruct((rows, MLA_HEADS * HEAD_PAD), BF16),
                   jax.ShapeDtypeStruct((rows, MLA_HEADS * V_DIM), BF16)],
        compiler_params=_params(1),
        name="mla_kv",
    )(x, cos_p, sin_p, gin, wd, glat, wuk, wuv)


def _q_kernel(x_ref, cos_ref, sin_ref, g_ref, wdq_ref, gq_ref, wq_ref, wrot_ref, q_ref):
    scale = 1.0 / math.sqrt(NOPE_DIM + ROPE_DIM)
    h = _rms(x_ref[...], g_ref[...]).astype(BF16)
    c_q = _rms(jnp.dot(h, wdq_ref[...], preferred_element_type=F32), gq_ref[...]).astype(BF16)
    q_main = jnp.dot(c_q, wq_ref[...], preferred_element_type=F32)
    q_rot = jnp.dot(c_q, wrot_ref[...], preferred_element_type=F32)
    cos_p, sin_p = cos_ref[...], sin_ref[...]
    for hd in range(MLA_HEADS):
        base = hd * HEAD_PAD
        q_ref[:, base:base + NOPE_DIM] = (q_main[:, base:base + NOPE_DIM] * scale).astype(BF16)
        rope = (q_main[:, base + NOPE_DIM:base + HEAD_PAD] * cos_p
                + q_rot[:, hd * LANES:(hd + 1) * LANES] * sin_p)
        q_ref[:, base + NOPE_DIM:base + HEAD_PAD] = (rope * scale).astype(BF16)


def _q(x, cos_p, sin_p, gain, wdq, gq, wq, wrot):
    rows = x.shape[0]

    def rows_of(width):
        return pl.BlockSpec((PROJ_ROWS, width), lambda i: (i, 0))

    return pl.pallas_call(
        _q_kernel,
        grid=(rows // PROJ_ROWS,),
        in_specs=[rows_of(D_MODEL), rows_of(LANES), rows_of(LANES), _resident(gain.shape),
                  _resident(wdq.shape), _resident(gq.shape), _resident(wq.shape),
                  _resident(wrot.shape)],
        out_specs=rows_of(MLA_HEADS * HEAD_PAD),
        out_shape=jax.ShapeDtypeStruct((rows, MLA_HEADS * HEAD_PAD), BF16),
        compiler_params=_params(1),
        name="mla_q",
    )(x, cos_p, sin_p, gain, wdq, gq, wq, wrot)


def _attn_kernel(q_ref, k_ref, v_ref, o_ref):
    row_chunk = lax.broadcasted_iota(jnp.int32, (ATT_Q, ATT_Q), 0) // CHUNK
    col_chunk = lax.broadcasted_iota(jnp.int32, (ATT_Q, ATT_Q), 1) // CHUNK
    diag_mask = col_chunk <= row_chunk
    neg = jnp.finfo(F32).min
    for i in range(SEQ // ATT_Q):
        q0, n_keys = i * ATT_Q, (i + 1) * ATT_Q
        s = lax.dot_general(q_ref[q0:q0 + ATT_Q, :], k_ref[0:n_keys, :],
                            (((1,), (1,)), ((), ())), preferred_element_type=F32)
        s_diag = jnp.where(diag_mask, s[:, q0:n_keys], neg)
        s = s_diag if i == 0 else jnp.concatenate([s[:, :q0], s_diag], axis=1)
        m = jnp.max(s, axis=-1, keepdims=True)
        p = jnp.exp(s - m)
        denom = jnp.sum(p, axis=-1, keepdims=True)
        o = jnp.dot(p.astype(BF16), v_ref[0:n_keys, :], preferred_element_type=F32)
        o_ref[q0:q0 + ATT_Q, :] = (o / denom).astype(BF16)


def _attention(q, k, v):
    return pl.pallas_call(
        _attn_kernel,
        grid=(BATCH, MLA_HEADS),
        in_specs=[pl.BlockSpec((None, SEQ, HEAD_PAD), lambda b, h: (b, 0, h)),
                  pl.BlockSpec((None, SEQ, HEAD_PAD), lambda b, h: (b, 0, h)),
                  pl.BlockSpec((None, SEQ, V_DIM), lambda b, h: (b, 0, h))],
        out_specs=pl.BlockSpec((None, SEQ, V_DIM), lambda b, h: (b, 0, h)),
        out_shape=jax.ShapeDtypeStruct((BATCH, SEQ, MLA_HEADS * V_DIM), BF16),
        compiler_params=_params(2),
        name="mla_attention",
    )(q, k, v)


def _rope_tables(positions):
    inv_freq = 1.0 / (ROPE_BASE ** (jnp.arange(0, ROPE_DIM, 2, dtype=F32) / ROPE_DIM))
    ang = positions.astype(F32)[..., None] * inv_freq
    cos, sin = jnp.cos(ang), jnp.sin(ang)
    zeros = jnp.zeros(cos.shape[:-1] + (LANES - ROPE_DIM,), F32)
    cos_p = jnp.concatenate([cos, cos, zeros], axis=-1)
    sin_p = jnp.concatenate([-sin, sin, zeros], axis=-1)
    return cos_p.reshape(-1, LANES), sin_p.reshape(-1, LANES)


def _rot_cols(w):
    half = ROPE_DIM // 2
    return jnp.concatenate([w[..., half:], w[..., :half]], axis=-1)


def _pad_cols(w, width):
    return jnp.pad(w, [(0, 0)] * (w.ndim - 1) + [(0, width - w.shape[-1])])


def kernel(x, positions, norm_gains, ffn_w_in, ffn_w_out, s5_lambda_re, s5_lambda_im, s5_log_step, s5_b_re, s5_b_im, s5_c_re, s5_c_im, s5_d, s5_w_glu, kv_in_norm, w_dkv, kv_latent_norm, w_ukv, mla_w_dq, mla_q_norm, mla_w_uq, mla_w_o, final_norm):
    assert x.shape == (BATCH, SEQ, D_MODEL) and norm_gains.shape[0] == 2
    rows = BATCH * SEQ
    gains = norm_gains.astype(F32).reshape(2, 3, 1, D_MODEL)
    w_in = ffn_w_in.astype(BF16)
    w_out = ffn_w_out.astype(BF16)
    cos_p, sin_p = _rope_tables(positions)

    xt = _ffn(x.astype(F32), gains[0, 0], w_in, w_out, 0, 0, relayout="bt_to_tb")
    wb, lam_tile, wc = _s5_weights(s5_lambda_re[0], s5_lambda_im[0], s5_log_step[0],
                                   s5_b_re[0], s5_b_im[0], s5_c_re[0], s5_c_im[0])
    xt = _s5(xt, gains[0, 1], wb, lam_tile, wc, s5_d[0].astype(F32).reshape(1, D_MODEL),
             s5_w_glu[0].astype(BF16))
    xb = _ffn(xt, gains[0, 2], w_in, w_out, 0, 1, relayout="tb_to_bt").reshape(rows, D_MODEL)

    w_rope = w_dkv[:, KV_LORA:]
    wd = jnp.concatenate([w_dkv[:, :KV_LORA], _pad_cols(w_rope, LANES),
                          _pad_cols(_rot_cols(w_rope), LANES)], axis=-1).astype(BF16)
    w_ukv_h = w_ukv.reshape(KV_LORA, MLA_HEADS, NOPE_DIM + V_DIM)
    wuk = w_ukv_h[:, :, :NOPE_DIM].reshape(KV_LORA, MLA_HEADS * NOPE_DIM).astype(BF16)
    wuv = w_ukv_h[:, :, NOPE_DIM:].reshape(KV_LORA, MLA_HEADS * V_DIM).astype(BF16)
    k, v = _kv(xb, cos_p, sin_p, kv_in_norm.astype(F32).reshape(1, D_MODEL), wd,
               kv_latent_norm.astype(F32).reshape(1, KV_LORA), wuk, wuv)

    xb = _ffn(xb, gains[1, 0], w_in, w_out, 1, 0)
    w_uq_h = mla_w_uq[0].reshape(Q_LORA, MLA_HEADS, NOPE_DIM + ROPE_DIM)
    wq = _pad_cols(w_uq_h, HEAD_PAD).reshape(Q_LORA, MLA_HEADS * HEAD_PAD).astype(BF16)
    wrot = _pad_cols(_rot_cols(w_uq_h[:, :, NOPE_DIM:]), LANES)
    wrot = wrot.reshape(Q_LORA, MLA_HEADS * LANES).astype(BF16)
    q = _q(xb, cos_p, sin_p, gains[1, 1], mla_w_dq[0].astype(BF16),
           mla_q_norm[0].astype(F32).reshape(1, Q_LORA), wq, wrot)
    o = _attention(q.reshape(BATCH, SEQ, -1), k.reshape(BATCH, SEQ, -1),
                   v.reshape(BATCH, SEQ, -1))
    out = _ffn(xb, gains[1, 2], w_in, w_out, 1, 1,
               attn=(o.reshape(rows, MLA_HEADS * V_DIM), mla_w_o[0].astype(BF16)),
               final_gain=final_norm.astype(F32).reshape(1, D_MODEL))
    return out.reshape(BATCH, SEQ, D_MODEL).astype(x.dtype)
```

```python
import functools
import math

import jax
import jax.numpy as jnp
from jax import lax
from jax.experimental import pallas as pl
from jax.experimental.pallas import tpu as pltpu

D_MODEL = 1024
BATCH = 8
SEQ = 2048
CHUNK = 64
EPS = 1e-6
D_FF = 2816
S5_GROUP = 16
S5_GROUPS = D_MODEL // S5_GROUP
S5_STATE = 64
MLA_HEADS = 8
NOPE_DIM = 128
ROPE_DIM = 64
V_DIM = 128
Q_LORA = 256
KV_LORA = 128
ROPE_BASE = 10000.0

LANES = 128
SUBLANES = 8
MXU_DIM = 256
VMEM_LIMIT_BYTES = 56 * 1024 * 1024

HEAD_PAD = 2 * LANES
FFN_STEPS = 64
FFN_ROWS = FFN_STEPS * BATCH
FFN_CHUNKS = ((0, 1024), (1024, 2048), (2048, 2816))
S5_STEPS = 64
S5_ROWS = S5_STEPS * BATCH
S5_KT = D_MODEL // MXU_DIM
S5_GPT = MXU_DIM // S5_GROUP
S5_MODES = S5_GPT * S5_STATE
S5_HALF = S5_MODES // 2
PROJ_ROWS = 512
ATT_Q = 256

BF16 = jnp.bfloat16
F32 = jnp.float32


def _rms(x, gain):
    return x * lax.rsqrt(jnp.mean(x * x, axis=-1, keepdims=True) + EPS) * gain


def _resident(shape):
    zeros = (0,) * len(shape)
    return pl.BlockSpec(shape, lambda *_: zeros, pipeline_mode=pl.Buffered(1))


def _params(n_grid_dims):
    return pltpu.CompilerParams(
        dimension_semantics=("arbitrary",) * n_grid_dims,
        vmem_limit_bytes=VMEM_LIMIT_BYTES)


def _ffn_kernel(*refs, relayout, with_attn, with_final):
    refs = list(refs)
    x_ref = refs.pop(0)
    if with_attn:
        o_ref, wo_ref = refs.pop(0), refs.pop(0)
    g_ref, win_ref, wout_ref = refs.pop(0), refs.pop(0), refs.pop(0)
    if with_final:
        gf_ref = refs.pop(0)
    out_ref = refs.pop(0)

    x = x_ref[...]
    if relayout == "bt_to_tb":
        x = jnp.swapaxes(x, 0, 1).reshape(FFN_ROWS, D_MODEL)
    elif relayout == "tb_to_bt":
        x = jnp.swapaxes(x.reshape(FFN_STEPS, BATCH, D_MODEL), 0, 1).reshape(FFN_ROWS, D_MODEL)
    if with_attn:
        x = x + jnp.dot(o_ref[...], wo_ref[...], preferred_element_type=F32)
    h = _rms(x, g_ref[...]).astype(BF16)
    acc = None
    for c0, c1 in FFN_CHUNKS:
        gate = jnp.dot(h, win_ref[:, c0:c1], preferred_element_type=F32)
        up = jnp.dot(h, win_ref[:, D_FF + c0:D_FF + c1], preferred_element_type=F32)
        a = (jax.nn.silu(gate) * up).astype(BF16)
        part = jnp.dot(a, wout_ref[c0:c1, :], preferred_element_type=F32)
        acc = part if acc is None else acc + part
    y = x + 0.5 * acc
    if with_final:
        y = _rms(y, gf_ref[...])
    out_ref[...] = y.reshape(out_ref.shape)


def _ffn(x, gain, w_in, w_out, layer, half, relayout=None, attn=None, final_gain=None):
    rows = BATCH * SEQ
    row_spec = pl.BlockSpec((FFN_ROWS, D_MODEL), lambda i: (i, 0))
    cube_spec = pl.BlockSpec((BATCH, FFN_STEPS, D_MODEL), lambda i: (0, i, 0))
    flat = jax.ShapeDtypeStruct((rows, D_MODEL), F32)
    cube = jax.ShapeDtypeStruct((BATCH, SEQ, D_MODEL), F32)
    in_spec = cube_spec if relayout == "bt_to_tb" else row_spec
    out_spec, out_shape = (cube_spec, cube) if relayout == "tb_to_bt" else (row_spec, flat)

    def weight(w):
        return pl.BlockSpec((None, None) + w.shape[2:], lambda i: (layer, half, 0, 0),
                            pipeline_mode=pl.Buffered(1))

    args, specs = [x], [in_spec]
    if attn is not None:
        o, w_o = attn
        args += [o, w_o]
        specs += [row_spec, _resident(w_o.shape)]
    args += [gain, w_in, w_out]
    specs += [_resident(gain.shape), weight(w_in), weight(w_out)]
    if final_gain is not None:
        args.append(final_gain)
        specs.append(_resident(final_gain.shape))
    return pl.pallas_call(
        functools.partial(_ffn_kernel, relayout=relayout, with_attn=attn is not None,
                          with_final=final_gain is not None),
        grid=(rows // FFN_ROWS,),
        in_specs=specs,
        out_specs=out_spec,
        out_shape=out_shape,
        compiler_params=_params(1),
        name="ffn",
    )(*args)


def _s5_kernel(x_ref, g_ref, wb_ref, lam_ref, wc_ref, d_ref, wglu_ref, out_ref,
               bu_ref, state_ref):
    @pl.when(pl.program_id(0) == 0)
    def _():
        state_ref[...] = jnp.zeros_like(state_ref)

    x = x_ref[...]
    h = _rms(x, g_ref[...])
    hb = h.astype(BF16)
    ys = []
    for kt in range(S5_KT):
        bu_ref[...] = jnp.dot(hb[:, kt * MXU_DIM:(kt + 1) * MXU_DIM], wb_ref[kt],
                              preferred_element_type=F32)
        for half in range(2):
            re0 = half * S5_HALF
            im0 = S5_MODES + half * S5_HALF
            lam_re = lam_ref[kt, :, re0:re0 + S5_HALF]
            lam_im = lam_ref[kt, :, im0:im0 + S5_HALF]

            def step(t, carry, re0=re0, im0=im0, lam_re=lam_re, lam_im=lam_im):
                s_re, s_im = carry
                r0 = pl.multiple_of(t * SUBLANES, SUBLANES)
                n_re = lam_re * s_re - lam_im * s_im + bu_ref[pl.ds(r0, SUBLANES), re0:re0 + S5_HALF]
                n_im = lam_re * s_im + lam_im * s_re + bu_ref[pl.ds(r0, SUBLANES), im0:im0 + S5_HALF]
                bu_ref[pl.ds(r0, SUBLANES), re0:re0 + S5_HALF] = n_re
                bu_ref[pl.ds(r0, SUBLANES), im0:im0 + S5_HALF] = n_im
                return n_re, n_im

            s_re, s_im = lax.fori_loop(
                0, S5_STEPS, step,
                (state_ref[kt, :, re0:re0 + S5_HALF], state_ref[kt, :, im0:im0 + S5_HALF]),
                unroll=8)
            state_ref[kt, :, re0:re0 + S5_HALF] = s_re
            state_ref[kt, :, im0:im0 + S5_HALF] = s_im
        ys.append(jnp.dot(bu_ref[...].astype(BF16), wc_ref[kt], preferred_element_type=F32))
    y = jnp.concatenate(ys, axis=1) + d_ref[...] * h
    y = jax.nn.gelu(y, approximate=True).astype(BF16)
    z = jnp.dot(y, wglu_ref[...], preferred_element_type=F32)
    out_ref[...] = x + z[:, :D_MODEL] * jax.nn.sigmoid(z[:, D_MODEL:])


def _s5(x, gain, wb, lam, wc, d_skip, w_glu):
    rows = x.shape[0]
    row_spec = pl.BlockSpec((S5_ROWS, D_MODEL), lambda i: (i, 0))
    return pl.pallas_call(
        _s5_kernel,
        grid=(rows // S5_ROWS,),
        in_specs=[row_spec, _resident(gain.shape), _resident(wb.shape), _resident(lam.shape),
                  _resident(wc.shape), _resident(d_skip.shape), _resident(w_glu.shape)],
        out_specs=row_spec,
        out_shape=jax.ShapeDtypeStruct((rows, D_MODEL), F32),
        scratch_shapes=[pltpu.VMEM((S5_ROWS, 2 * S5_MODES), F32),
                        pltpu.VMEM((S5_KT, SUBLANES, 2 * S5_MODES), F32)],
        compiler_params=_params(1),
        name="s5_mixer",
    )(x, gain, wb, lam, wc, d_skip, w_glu)


def _s5_disc_kernel(lam_re_ref, lam_im_ref, lam_re_rep_ref, lam_im_rep_ref, log_step_ref,
                    b_re_ref, b_im_ref, lbar_re_ref, lbar_im_ref, bbar_re_ref, bbar_im_ref):
    step = jnp.exp(log_step_ref[...])

    def lam_bar(lam_re, lam_im):
        mag = jnp.exp(lam_re * step)
        return mag * jnp.cos(lam_im * step), mag * jnp.sin(lam_im * step)

    lbar_re_ref[...], lbar_im_ref[...] = lam_bar(lam_re_ref[...], lam_im_ref[...])
    lam_re, lam_im = lam_re_rep_ref[...], lam_im_rep_ref[...]
    lb_re, lb_im = lam_bar(lam_re, lam_im)
    n_re, n_im = lb_re - 1.0, lb_im
    den = lam_re * lam_re + lam_im * lam_im
    f_re = (n_re * lam_re + n_im * lam_im) / den
    f_im = (n_im * lam_re - n_re * lam_im) / den
    b_re, b_im = b_re_ref[...], b_im_ref[...]
    bbar_re_ref[...] = f_re * b_re - f_im * b_im
    bbar_im_ref[...] = f_re * b_im + f_im * b_re


def _s5_weights(lam_re, lam_im, log_step, b_re, b_im, c_re, c_im):
    pc = S5_STATE * S5_GROUP
    gp = jax.ShapeDtypeStruct((S5_GROUPS, S5_STATE), F32)
    gpc = jax.ShapeDtypeStruct((S5_GROUPS, pc), F32)
    lbar_re, lbar_im, bbar_re, bbar_im = pl.pallas_call(
        _s5_disc_kernel, out_shape=[gp, gp, gpc, gpc], name="s5_discretise",
    )(lam_re.astype(F32), lam_im.astype(F32),
      jnp.repeat(lam_re.astype(F32), S5_GROUP, axis=1), jnp.repeat(lam_im.astype(F32), S5_GROUP, axis=1),
      log_step.astype(F32).reshape(S5_GROUPS, 1),
      b_re.astype(F32).reshape(S5_GROUPS, pc), b_im.astype(F32).reshape(S5_GROUPS, pc))
    eye = jnp.eye(S5_GPT, dtype=F32)

    def pack_b(part):
        part = part.reshape(S5_KT, S5_GPT, S5_STATE, S5_GROUP)
        return jnp.einsum('kgpc,gh->kgchp', part, eye).reshape(S5_KT, MXU_DIM, S5_MODES)

    def pack_c(part):
        part = part.reshape(S5_KT, S5_GPT, S5_GROUP, S5_STATE)
        return jnp.einsum('kgcp,gh->kgphc', part, eye).reshape(S5_KT, S5_MODES, MXU_DIM)

    wb = jnp.concatenate([pack_b(bbar_re), pack_b(bbar_im)], axis=-1)
    wc = jnp.concatenate([pack_c(c_re.astype(F32)), -pack_c(c_im.astype(F32))], axis=1)
    lam_row = jnp.concatenate([lbar_re.reshape(S5_KT, S5_MODES),
                               lbar_im.reshape(S5_KT, S5_MODES)], axis=-1)
    lam_tile = jnp.broadcast_to(lam_row[:, None, :], (S5_KT, SUBLANES, 2 * S5_MODES))
    return wb.astype(BF16), lam_tile, wc.astype(BF16)


def _kv_kernel(x_ref, cos_ref, sin_ref, gin_ref, wd_ref, glat_ref, wuk_ref, wuv_ref,
               k_ref, v_ref):
    h = _rms(x_ref[...], gin_ref[...]).astype(BF16)
    kv_a = jnp.dot(h, wd_ref[...], preferred_element_type=F32)
    c_kv = _rms(kv_a[:, :KV_LORA], glat_ref[...]).astype(BF16)
    k_rope = (kv_a[:, LANES:2 * LANES] * cos_ref[...]
              + kv_a[:, 2 * LANES:3 * LANES] * sin_ref[...]).astype(BF16)
    k_nope = jnp.dot(c_kv, wuk_ref[...], preferred_element_type=F32).astype(BF16)
    v_ref[...] = jnp.dot(c_kv, wuv_ref[...], preferred_element_type=F32).astype(BF16)
    for hd in range(MLA_HEADS):
        k_ref[:, hd * HEAD_PAD:hd * HEAD_PAD + NOPE_DIM] = k_nope[:, hd * NOPE_DIM:(hd + 1) * NOPE_DIM]
        k_ref[:, hd * HEAD_PAD + NOPE_DIM:(hd + 1) * HEAD_PAD] = k_rope


def _kv(x, cos_p, sin_p, gin, wd, glat, wuk, wuv):
    rows = x.shape[0]

    def rows_of(width):
        return pl.BlockSpec((PROJ_ROWS, width), lambda i: (i, 0))

    return pl.pallas_call(
        _kv_kernel,
        grid=(rows // PROJ_ROWS,),
        in_specs=[rows_of(D_MODEL), rows_of(LANES), rows_of(LANES), _resident(gin.shape),
                  _resident(wd.shape), _resident(glat.shape), _resident(wuk.shape),
                  _resident(wuv.shape)],
        out_specs=[rows_of(MLA_HEADS * HEAD_PAD), rows_of(MLA_HEADS * V_DIM)],
        out_shape=[jax.ShapeDtypeStruct((rows, MLA_HEADS * HEAD_PAD), BF16),
                   jax.ShapeDtypeStruct((rows, MLA_HEADS * V_DIM), BF16)],
        compiler_params=_params(1),
        name="mla_kv",
    )(x, cos_p, sin_p, gin, wd, glat, wuk, wuv)


def _q_kernel(x_ref, cos_ref, sin_ref, g_ref, wdq_ref, gq_ref, wq_ref, wrot_ref, q_ref):
    scale = math.log2(math.e) / math.sqrt(NOPE_DIM + ROPE_DIM)
    h = _rms(x_ref[...], g_ref[...]).astype(BF16)
    c_q = _rms(jnp.dot(h, wdq_ref[...], preferred_element_type=F32), gq_ref[...]).astype(BF16)
    q_main = jnp.dot(c_q, wq_ref[...], preferred_element_type=F32)
    q_rot = jnp.dot(c_q, wrot_ref[...], preferred_element_type=F32)
    cos_p, sin_p = cos_ref[...], sin_ref[...]
    for hd in range(MLA_HEADS):
        base = hd * HEAD_PAD
        q_ref[:, base:base + NOPE_DIM] = (q_main[:, base:base + NOPE_DIM] * scale).astype(BF16)
        rope = (q_main[:, base + NOPE_DIM:base + HEAD_PAD] * cos_p
                + q_rot[:, hd * LANES:(hd + 1) * LANES] * sin_p)
        q_ref[:, base + NOPE_DIM:base + HEAD_PAD] = (rope * scale).astype(BF16)


def _q(x, cos_p, sin_p, gain, wdq, gq, wq, wrot):
    rows = x.shape[0]

    def rows_of(width):
        return pl.BlockSpec((PROJ_ROWS, width), lambda i: (i, 0))

    return pl.pallas_call(
        _q_kernel,
        grid=(rows // PROJ_ROWS,),
        in_specs=[rows_of(D_MODEL), rows_of(LANES), rows_of(LANES), _resident(gain.shape),
                  _resident(wdq.shape), _resident(gq.shape), _resident(wq.shape),
                  _resident(wrot.shape)],
        out_specs=rows_of(MLA_HEADS * HEAD_PAD),
        out_shape=jax.ShapeDtypeStruct((rows, MLA_HEADS * HEAD_PAD), BF16),
        compiler_params=_params(1),
        name="mla_q",
    )(x, cos_p, sin_p, gain, wdq, gq, wq, wrot)


def _attn_kernel(q_ref, k_ref, v_ref, o_ref):
    row_chunk = lax.broadcasted_iota(jnp.int32, (ATT_Q, ATT_Q), 0) // CHUNK
    col_chunk = lax.broadcasted_iota(jnp.int32, (ATT_Q, ATT_Q), 1) // CHUNK
    diag_mask = col_chunk <= row_chunk
    neg = jnp.finfo(F32).min
    v_ones = jnp.concatenate([v_ref[...], jnp.ones((SEQ, V_DIM), BF16)], axis=1)
    for i in reversed(range(SEQ // ATT_Q)):
        q0, n_keys = i * ATT_Q, (i + 1) * ATT_Q
        s = lax.dot_general(q_ref[q0:q0 + ATT_Q, :], k_ref[0:n_keys, :],
                            (((1,), (1,)), ((), ())), preferred_element_type=F32)
        s_diag = jnp.where(diag_mask, s[:, q0:n_keys], neg)
        s = s_diag if i == 0 else jnp.concatenate([s[:, :q0], s_diag], axis=1)
        m = jnp.max(s, axis=-1, keepdims=True)
        p = jnp.exp2(s - m).astype(BF16)
        o = jnp.dot(p, v_ones[0:n_keys, :], preferred_element_type=F32)
        o_ref[q0:q0 + ATT_Q, :] = (o[:, :V_DIM] / o[:, V_DIM:]).astype(BF16)


def _attention(q, k, v):
    return pl.pallas_call(
        _attn_kernel,
        grid=(BATCH, MLA_HEADS),
        in_specs=[pl.BlockSpec((None, SEQ, HEAD_PAD), lambda b, h: (b, 0, h)),
                  pl.BlockSpec((None, SEQ, HEAD_PAD), lambda b, h: (b, 0, h)),
                  pl.BlockSpec((None, SEQ, V_DIM), lambda b, h: (b, 0, h))],
        out_specs=pl.BlockSpec((None, SEQ, V_DIM), lambda b, h: (b, 0, h)),
        out_shape=jax.ShapeDtypeStruct((BATCH, SEQ, MLA_HEADS * V_DIM), BF16),
        compiler_params=_params(2),
        name="mla_attention",
    )(q, k, v)


def _rope_tables(positions):
    inv_freq = 1.0 / (ROPE_BASE ** (jnp.arange(0, ROPE_DIM, 2, dtype=F32) / ROPE_DIM))
    ang = positions.astype(F32)[..., None] * inv_freq
    cos, sin = jnp.cos(ang), jnp.sin(ang)
    zeros = jnp.zeros(cos.shape[:-1] + (LANES - ROPE_DIM,), F32)
    cos_p = jnp.concatenate([cos, cos, zeros], axis=-1)
    sin_p = jnp.concatenate([-sin, sin, zeros], axis=-1)
    return cos_p.reshape(-1, LANES), sin_p.reshape(-1, LANES)


def _rot_cols(w):
    half = ROPE_DIM // 2
    return jnp.concatenate([w[..., half:], w[..., :half]], axis=-1)


def _pad_cols(w, width):
    return jnp.pad(w, [(0, 0)] * (w.ndim - 1) + [(0, width - w.shape[-1])])


def kernel(x, positions, norm_gains, ffn_w_in, ffn_w_out, s5_lambda_re, s5_lambda_im, s5_log_step, s5_b_re, s5_b_im, s5_c_re, s5_c_im, s5_d, s5_w_glu, kv_in_norm, w_dkv, kv_latent_norm, w_ukv, mla_w_dq, mla_q_norm, mla_w_uq, mla_w_o, final_norm):
    assert x.shape == (BATCH, SEQ, D_MODEL) and norm_gains.shape[0] == 2
    rows = BATCH * SEQ
    gains = norm_gains.astype(F32).reshape(2, 3, 1, D_MODEL)
    w_in = ffn_w_in.astype(BF16)
    w_out = ffn_w_out.astype(BF16)
    cos_p, sin_p = _rope_tables(positions)

    xt = _ffn(x.astype(F32), gains[0, 0], w_in, w_out, 0, 0, relayout="bt_to_tb")
    wb, lam_tile, wc = _s5_weights(s5_lambda_re[0], s5_lambda_im[0], s5_log_step[0],
                                   s5_b_re[0], s5_b_im[0], s5_c_re[0], s5_c_im[0])
    xt = _s5(xt, gains[0, 1], wb, lam_tile, wc, s5_d[0].astype(F32).reshape(1, D_MODEL),
             s5_w_glu[0].astype(BF16))
    xb = _ffn(xt, gains[0, 2], w_in, w_out, 0, 1, relayout="tb_to_bt").reshape(rows, D_MODEL)

    w_rope = w_dkv[:, KV_LORA:]
    wd = jnp.concatenate([w_dkv[:, :KV_LORA], _pad_cols(w_rope, LANES),
                          _pad_cols(_rot_cols(w_rope), LANES)], axis=-1).astype(BF16)
    w_ukv_h = w_ukv.reshape(KV_LORA, MLA_HEADS, NOPE_DIM + V_DIM)
    wuk = w_ukv_h[:, :, :NOPE_DIM].reshape(KV_LORA, MLA_HEADS * NOPE_DIM).astype(BF16)
    wuv = w_ukv_h[:, :, NOPE_DIM:].reshape(KV_LORA, MLA_HEADS * V_DIM).astype(BF16)
    k, v = _kv(xb, cos_p, sin_p, kv_in_norm.astype(F32).reshape(1, D_MODEL), wd,
               kv_latent_norm.astype(F32).reshape(1, KV_LORA), wuk, wuv)

    xb = _ffn(xb, gains[1, 0], w_in, w_out, 1, 0)
    w_uq_h = mla_w_uq[0].reshape(Q_LORA, MLA_HEADS, NOPE_DIM + ROPE_DIM)
    wq = _pad_cols(w_uq_h, HEAD_PAD).reshape(Q_LORA, MLA_HEADS * HEAD_PAD).astype(BF16)
    wrot = _pad_cols(_rot_cols(w_uq_h[:, :, NOPE_DIM:]), LANES)
    wrot = wrot.reshape(Q_LORA, MLA_HEADS * LANES).astype(BF16)
    q = _q(xb, cos_p, sin_p, gains[1, 1], mla_w_dq[0].astype(BF16),
           mla_q_norm[0].astype(F32).reshape(1, Q_LORA), wq, wrot)
    o = _attention(q.reshape(BATCH, SEQ, -1), k.reshape(BATCH, SEQ, -1),
                   v.reshape(BATCH, SEQ, -1))
    out = _ffn(xb, gains[1, 2], w_in, w_out, 1, 1,
               attn=(o.reshape(rows, MLA_HEADS * V_DIM), mla_w_o[0].astype(BF16)),
               final_gain=final_norm.astype(F32).reshape(1, D_MODEL))
    return out.reshape(BATCH, SEQ, D_MODEL).astype(x.dtype)
```

```python
import functools
import math

import jax
import jax.numpy as jnp
from jax import lax
from jax.experimental import pallas as pl
from jax.experimental.pallas import tpu as pltpu

D_MODEL = 1024
BATCH = 8
SEQ = 2048
CHUNK = 64
EPS = 1e-6
D_FF = 2816
S5_GROUP = 16
S5_GROUPS = D_MODEL // S5_GROUP
S5_STATE = 64
MLA_HEADS = 8
NOPE_DIM = 128
ROPE_DIM = 64
V_DIM = 128
Q_LORA = 256
KV_LORA = 128
ROPE_BASE = 10000.0

LANES = 128
SUBLANES = 8
MXU_DIM = 256
VMEM_LIMIT_BYTES = 56 * 1024 * 1024

HEAD_PAD = 2 * LANES
FFN_STEPS = 64
FFN_ROWS = FFN_STEPS * BATCH
FFN_CHUNKS = ((0, 1024), (1024, 2048), (2048, 2816))
S5_STEPS = 64
S5_ROWS = S5_STEPS * BATCH
S5_KT = D_MODEL // MXU_DIM
S5_GPT = MXU_DIM // S5_GROUP
S5_MODES = S5_GPT * S5_STATE
S5_HALF = S5_MODES // 2
PROJ_ROWS = 512
ATT_Q = 256

BF16 = jnp.bfloat16
F32 = jnp.float32


def _rms(x, gain):
    return x * lax.rsqrt(jnp.mean(x * x, axis=-1, keepdims=True) + EPS) * gain


def _resident(shape):
    zeros = (0,) * len(shape)
    return pl.BlockSpec(shape, lambda *_: zeros, pipeline_mode=pl.Buffered(1))


def _params(n_grid_dims):
    return pltpu.CompilerParams(
        dimension_semantics=("arbitrary",) * n_grid_dims,
        vmem_limit_bytes=VMEM_LIMIT_BYTES)


def _ffn_kernel(*refs, relayout, with_attn, with_final):
    refs = list(refs)
    x_ref = refs.pop(0)
    if with_attn:
        o_ref, wo_ref = refs.pop(0), refs.pop(0)
    g_ref, win_ref, wout_ref = refs.pop(0), refs.pop(0), refs.pop(0)
    if with_final:
        gf_ref = refs.pop(0)
    out_ref = refs.pop(0)

    x = x_ref[...]
    if relayout == "bt_to_tb":
        x = jnp.swapaxes(x, 0, 1).reshape(FFN_ROWS, D_MODEL)
    elif relayout == "tb_to_bt":
        x = jnp.swapaxes(x.reshape(FFN_STEPS, BATCH, D_MODEL), 0, 1).reshape(FFN_ROWS, D_MODEL)
    if with_attn:
        x = x + jnp.dot(o_ref[...], wo_ref[...], preferred_element_type=F32)
    h = _rms(x, g_ref[...]).astype(BF16)
    acc = None
    for c0, c1 in FFN_CHUNKS:
        gate = jnp.dot(h, win_ref[:, c0:c1], preferred_element_type=F32)
        up = jnp.dot(h, win_ref[:, D_FF + c0:D_FF + c1], preferred_element_type=F32)
        a = (jax.nn.silu(gate) * up).astype(BF16)
        part = jnp.dot(a, wout_ref[c0:c1, :], preferred_element_type=F32)
        acc = part if acc is None else acc + part
    y = x + 0.5 * acc
    if with_final:
        y = _rms(y, gf_ref[...])
    out_ref[...] = y.reshape(out_ref.shape)


def _ffn(x, gain, w_in, w_out, layer, half, relayout=None, attn=None, final_gain=None):
    rows = BATCH * SEQ
    row_spec = pl.BlockSpec((FFN_ROWS, D_MODEL), lambda i: (i, 0))
    cube_spec = pl.BlockSpec((BATCH, FFN_STEPS, D_MODEL), lambda i: (0, i, 0))
    flat = jax.ShapeDtypeStruct((rows, D_MODEL), F32)
    cube = jax.ShapeDtypeStruct((BATCH, SEQ, D_MODEL), F32)
    in_spec = cube_spec if relayout == "bt_to_tb" else row_spec
    out_spec, out_shape = (cube_spec, cube) if relayout == "tb_to_bt" else (row_spec, flat)

    def weight(w):
        return pl.BlockSpec((None, None) + w.shape[2:], lambda i: (layer, half, 0, 0),
                            pipeline_mode=pl.Buffered(1))

    args, specs = [x], [in_spec]
    if attn is not None:
        o, w_o = attn
        args += [o, w_o]
        specs += [row_spec, _resident(w_o.shape)]
    args += [gain, w_in, w_out]
    specs += [_resident(gain.shape), weight(w_in), weight(w_out)]
    if final_gain is not None:
        args.append(final_gain)
        specs.append(_resident(final_gain.shape))
    return pl.pallas_call(
        functools.partial(_ffn_kernel, relayout=relayout, with_attn=attn is not None,
                          with_final=final_gain is not None),
        grid=(rows // FFN_ROWS,),
        in_specs=specs,
        out_specs=out_spec,
        out_shape=out_shape,
        compiler_params=_params(1),
        name="ffn",
    )(*args)


def _s5_kernel(x_ref, g_ref, wb_ref, lam_ref, wc_ref, d_ref, wglu_ref, out_ref,
               bu_ref, state_ref):
    @pl.when(pl.program_id(0) == 0)
    def _():
        state_ref[...] = jnp.zeros_like(state_ref)

    x = x_ref[...]
    h = _rms(x, g_ref[...])
    hb = h.astype(BF16)

    def project_in(kt):
        bu_ref[kt] = jnp.dot(hb[:, kt * MXU_DIM:(kt + 1) * MXU_DIM], wb_ref[kt],
                             preferred_element_type=F32)

    def scan(kt):
        cols = [(half * S5_HALF, S5_MODES + half * S5_HALF) for half in range(2)]
        state = [(state_ref[kt, :, re0:re0 + S5_HALF], state_ref[kt, :, im0:im0 + S5_HALF])
                 for re0, im0 in cols]
        for t in range(S5_STEPS):
            r0 = t * SUBLANES
            for half, (re0, im0) in enumerate(cols):
                lam_re = lam_ref[kt, :, re0:re0 + S5_HALF]
                lam_im = lam_ref[kt, :, im0:im0 + S5_HALF]
                s_re, s_im = state[half]
                n_re = lam_re * s_re - lam_im * s_im + bu_ref[kt, r0:r0 + SUBLANES, re0:re0 + S5_HALF]
                n_im = lam_re * s_im + lam_im * s_re + bu_ref[kt, r0:r0 + SUBLANES, im0:im0 + S5_HALF]
                bu_ref[kt, r0:r0 + SUBLANES, re0:re0 + S5_HALF] = n_re
                bu_ref[kt, r0:r0 + SUBLANES, im0:im0 + S5_HALF] = n_im
                state[half] = (n_re, n_im)
        for (re0, im0), (s_re, s_im) in zip(cols, state):
            state_ref[kt, :, re0:re0 + S5_HALF] = s_re
            state_ref[kt, :, im0:im0 + S5_HALF] = s_im

    ys = []
    project_in(0)
    for kt in range(S5_KT):
        if kt + 1 < S5_KT:
            project_in(kt + 1)
        scan(kt)
        ys.append(jnp.dot(bu_ref[kt].astype(BF16), wc_ref[kt], preferred_element_type=F32))
    y = jnp.concatenate(ys, axis=1) + d_ref[...] * h
    y = jax.nn.gelu(y, approximate=True).astype(BF16)
    z = jnp.dot(y, wglu_ref[...], preferred_element_type=F32)
    out_ref[...] = x + z[:, :D_MODEL] * jax.nn.sigmoid(z[:, D_MODEL:])


def _s5(x, gain, wb, lam, wc, d_skip, w_glu):
    rows = x.shape[0]
    row_spec = pl.BlockSpec((S5_ROWS, D_MODEL), lambda i: (i, 0))
    return pl.pallas_call(
        _s5_kernel,
        grid=(rows // S5_ROWS,),
        in_specs=[row_spec, _resident(gain.shape), _resident(wb.shape), _resident(lam.shape),
                  _resident(wc.shape), _resident(d_skip.shape), _resident(w_glu.shape)],
        out_specs=row_spec,
        out_shape=jax.ShapeDtypeStruct((rows, D_MODEL), F32),
        scratch_shapes=[pltpu.VMEM((S5_KT, S5_ROWS, 2 * S5_MODES), F32),
                        pltpu.VMEM((S5_KT, SUBLANES, 2 * S5_MODES), F32)],
        compiler_params=_params(1),
        name="s5_mixer",
    )(x, gain, wb, lam, wc, d_skip, w_glu)


def _s5_disc_kernel(lam_re_ref, lam_im_ref, lam_re_rep_ref, lam_im_rep_ref, log_step_ref,
                    b_re_ref, b_im_ref, lbar_re_ref, lbar_im_ref, bbar_re_ref, bbar_im_ref):
    step = jnp.exp(log_step_ref[...])

    def lam_bar(lam_re, lam_im):
        mag = jnp.exp(lam_re * step)
        return mag * jnp.cos(lam_im * step), mag * jnp.sin(lam_im * step)

    lbar_re_ref[...], lbar_im_ref[...] = lam_bar(lam_re_ref[...], lam_im_ref[...])
    lam_re, lam_im = lam_re_rep_ref[...], lam_im_rep_ref[...]
    lb_re, lb_im = lam_bar(lam_re, lam_im)
    n_re, n_im = lb_re - 1.0, lb_im
    den = lam_re * lam_re + lam_im * lam_im
    f_re = (n_re * lam_re + n_im * lam_im) / den
    f_im = (n_im * lam_re - n_re * lam_im) / den
    b_re, b_im = b_re_ref[...], b_im_ref[...]
    bbar_re_ref[...] = f_re * b_re - f_im * b_im
    bbar_im_ref[...] = f_re * b_im + f_im * b_re


def _s5_weights(lam_re, lam_im, log_step, b_re, b_im, c_re, c_im):
    pc = S5_STATE * S5_GROUP
    gp = jax.ShapeDtypeStruct((S5_GROUPS, S5_STATE), F32)
    gpc = jax.ShapeDtypeStruct((S5_GROUPS, pc), F32)
    lbar_re, lbar_im, bbar_re, bbar_im = pl.pallas_call(
        _s5_disc_kernel, out_shape=[gp, gp, gpc, gpc], name="s5_discretise",
    )(lam_re.astype(F32), lam_im.astype(F32),
      jnp.repeat(lam_re.astype(F32), S5_GROUP, axis=1), jnp.repeat(lam_im.astype(F32), S5_GROUP, axis=1),
      log_step.astype(F32).reshape(S5_GROUPS, 1),
      b_re.astype(F32).reshape(S5_GROUPS, pc), b_im.astype(F32).reshape(S5_GROUPS, pc))
    eye = jnp.eye(S5_GPT, dtype=F32)

    def pack_b(part):
        part = part.reshape(S5_KT, S5_GPT, S5_STATE, S5_GROUP)
        return jnp.einsum('kgpc,gh->kgchp', part, eye).reshape(S5_KT, MXU_DIM, S5_MODES)

    def pack_c(part):
        part = part.reshape(S5_KT, S5_GPT, S5_GROUP, S5_STATE)
        return jnp.einsum('kgcp,gh->kgphc', part, eye).reshape(S5_KT, S5_MODES, MXU_DIM)

    wb = jnp.concatenate([pack_b(bbar_re), pack_b(bbar_im)], axis=-1)
    wc = jnp.concatenate([pack_c(c_re.astype(F32)), -pack_c(c_im.astype(F32))], axis=1)
    lam_row = jnp.concatenate([lbar_re.reshape(S5_KT, S5_MODES),
                               lbar_im.reshape(S5_KT, S5_MODES)], axis=-1)
    lam_tile = jnp.broadcast_to(lam_row[:, None, :], (S5_KT, SUBLANES, 2 * S5_MODES))
    return wb.astype(BF16), lam_tile, wc.astype(BF16)


def _kv_kernel(x_ref, cos_ref, sin_ref, gin_ref, wd_ref, glat_ref, wuk_ref, wuv_ref,
               k_ref, v_ref):
    h = _rms(x_ref[...], gin_ref[...]).astype(BF16)
    kv_a = jnp.dot(h, wd_ref[...], preferred_element_type=F32)
    c_kv = _rms(kv_a[:, :KV_LORA], glat_ref[...]).astype(BF16)
    k_rope = (kv_a[:, LANES:2 * LANES] * cos_ref[...]
              + kv_a[:, 2 * LANES:3 * LANES] * sin_ref[...]).astype(BF16)
    k_nope = jnp.dot(c_kv, wuk_ref[...], preferred_element_type=F32).astype(BF16)
    v_ref[...] = jnp.dot(c_kv, wuv_ref[...], preferred_element_type=F32).astype(BF16)
    for hd in range(MLA_HEADS):
        k_ref[:, hd * HEAD_PAD:hd * HEAD_PAD + NOPE_DIM] = k_nope[:, hd * NOPE_DIM:(hd + 1) * NOPE_DIM]
        k_ref[:, hd * HEAD_PAD + NOPE_DIM:(hd + 1) * HEAD_PAD] = k_rope


def _kv(x, cos_p, sin_p, gin, wd, glat, wuk, wuv):
    rows = x.shape[0]

    def rows_of(width):
        return pl.BlockSpec((PROJ_ROWS, width), lambda i: (i, 0))

    return pl.pallas_call(
        _kv_kernel,
        grid=(rows // PROJ_ROWS,),
        in_specs=[rows_of(D_MODEL), rows_of(LANES), rows_of(LANES), _resident(gin.shape),
                  _resident(wd.shape), _resident(glat.shape), _resident(wuk.shape),
                  _resident(wuv.shape)],
        out_specs=[rows_of(MLA_HEADS * HEAD_PAD), rows_of(MLA_HEADS * V_DIM)],
        out_shape=[jax.ShapeDtypeStruct((rows, MLA_HEADS * HEAD_PAD), BF16),
                   jax.ShapeDtypeStruct((rows, MLA_HEADS * V_DIM), BF16)],
        compiler_params=_params(1),
        name="mla_kv",
    )(x, cos_p, sin_p, gin, wd, glat, wuk, wuv)


def _q_kernel(x_ref, cos_ref, sin_ref, g_ref, wdq_ref, gq_ref, wq_ref, wrot_ref, q_ref):
    scale = math.log2(math.e) / math.sqrt(NOPE_DIM + ROPE_DIM)
    h = _rms(x_ref[...], g_ref[...]).astype(BF16)
    c_q = _rms(jnp.dot(h, wdq_ref[...], preferred_element_type=F32), gq_ref[...]).astype(BF16)
    q_main = jnp.dot(c_q, wq_ref[...], preferred_element_type=F32)
    q_rot = jnp.dot(c_q, wrot_ref[...], preferred_element_type=F32)
    cos_p, sin_p = cos_ref[...], sin_ref[...]
    for hd in range(MLA_HEADS):
        base = hd * HEAD_PAD
        q_ref[:, base:base + NOPE_DIM] = (q_main[:, base:base + NOPE_DIM] * scale).astype(BF16)
        rope = (q_main[:, base + NOPE_DIM:base + HEAD_PAD] * cos_p
                + q_rot[:, hd * LANES:(hd + 1) * LANES] * sin_p)
        q_ref[:, base + NOPE_DIM:base + HEAD_PAD] = (rope * scale).astype(BF16)


def _q(x, cos_p, sin_p, gain, wdq, gq, wq, wrot):
    rows = x.shape[0]

    def rows_of(width):
        return pl.BlockSpec((PROJ_ROWS, width), lambda i: (i, 0))

    return pl.pallas_call(
        _q_kernel,
        grid=(rows // PROJ_ROWS,),
        in_specs=[rows_of(D_MODEL), rows_of(LANES), rows_of(LANES), _resident(gain.shape),
                  _resident(wdq.shape), _resident(gq.shape), _resident(wq.shape),
                  _resident(wrot.shape)],
        out_specs=rows_of(MLA_HEADS * HEAD_PAD),
        out_shape=jax.ShapeDtypeStruct((rows, MLA_HEADS * HEAD_PAD), BF16),
        compiler_params=_params(1),
        name="mla_q",
    )(x, cos_p, sin_p, gain, wdq, gq, wq, wrot)


def _attn_kernel(q_ref, k_ref, v_ref, o_ref):
    row_chunk = lax.broadcasted_iota(jnp.int32, (ATT_Q, ATT_Q), 0) // CHUNK
    col_chunk = lax.broadcasted_iota(jnp.int32, (ATT_Q, ATT_Q), 1) // CHUNK
    diag_mask = col_chunk <= row_chunk
    neg = jnp.finfo(F32).min
    v_ones = jnp.concatenate([v_ref[...], jnp.ones((SEQ, V_DIM), BF16)], axis=1)
    for i in reversed(range(SEQ // ATT_Q)):
        q0, n_keys = i * ATT_Q, (i + 1) * ATT_Q
        s = lax.dot_general(q_ref[q0:q0 + ATT_Q, :], k_ref[0:n_keys, :],
                            (((1,), (1,)), ((), ())), preferred_element_type=F32)
        s_diag = jnp.where(diag_mask, s[:, q0:n_keys], neg)
        s = s_diag if i == 0 else jnp.concatenate([s[:, :q0], s_diag], axis=1)
        m = jnp.max(s, axis=-1, keepdims=True)
        p = jnp.exp2(s - m).astype(BF16)
        o = jnp.dot(p, v_ones[0:n_keys, :], preferred_element_type=F32)
        o_ref[q0:q0 + ATT_Q, :] = (o[:, :V_DIM] / o[:, V_DIM:]).astype(BF16)


def _attention(q, k, v):
    return pl.pallas_call(
        _attn_kernel,
        grid=(BATCH, MLA_HEADS),
        in_specs=[pl.BlockSpec((None, SEQ, HEAD_PAD), lambda b, h: (b, 0, h)),
                  pl.BlockSpec((None, SEQ, HEAD_PAD), lambda b, h: (b, 0, h)),
                  pl.BlockSpec((None, SEQ, V_DIM), lambda b, h: (b, 0, h))],
        out_specs=pl.BlockSpec((None, SEQ, V_DIM), lambda b, h: (b, 0, h)),
        out_shape=jax.ShapeDtypeStruct((BATCH, SEQ, MLA_HEADS * V_DIM), BF16),
        compiler_params=_params(2),
        name="mla_attention",
    )(q, k, v)


def _rope_tables(positions):
    inv_freq = 1.0 / (ROPE_BASE ** (jnp.arange(0, ROPE_DIM, 2, dtype=F32) / ROPE_DIM))
    ang = positions.astype(F32)[..., None] * inv_freq
    cos, sin = jnp.cos(ang), jnp.sin(ang)
    zeros = jnp.zeros(cos.shape[:-1] + (LANES - ROPE_DIM,), F32)
    cos_p = jnp.concatenate([cos, cos, zeros], axis=-1)
    sin_p = jnp.concatenate([-sin, sin, zeros], axis=-1)
    return cos_p.reshape(-1, LANES), sin_p.reshape(-1, LANES)


def _rot_cols(w):
    half = ROPE_DIM // 2
    return jnp.concatenate([w[..., half:], w[..., :half]], axis=-1)


def _pad_cols(w, width):
    return jnp.pad(w, [(0, 0)] * (w.ndim - 1) + [(0, width - w.shape[-1])])


def kernel(x, positions, norm_gains, ffn_w_in, ffn_w_out, s5_lambda_re, s5_lambda_im, s5_log_step, s5_b_re, s5_b_im, s5_c_re, s5_c_im, s5_d, s5_w_glu, kv_in_norm, w_dkv, kv_latent_norm, w_ukv, mla_w_dq, mla_q_norm, mla_w_uq, mla_w_o, final_norm):
    assert x.shape == (BATCH, SEQ, D_MODEL) and norm_gains.shape[0] == 2
    rows = BATCH * SEQ
    gains = norm_gains.astype(F32).reshape(2, 3, 1, D_MODEL)
    w_in = ffn_w_in.astype(BF16)
    w_out = ffn_w_out.astype(BF16)
    cos_p, sin_p = _rope_tables(positions)

    xt = _ffn(x.astype(F32), gains[0, 0], w_in, w_out, 0, 0, relayout="bt_to_tb")
    wb, lam_tile, wc = _s5_weights(s5_lambda_re[0], s5_lambda_im[0], s5_log_step[0],
                                   s5_b_re[0], s5_b_im[0], s5_c_re[0], s5_c_im[0])
    xt = _s5(xt, gains[0, 1], wb, lam_tile, wc, s5_d[0].astype(F32).reshape(1, D_MODEL),
             s5_w_glu[0].astype(BF16))
    xb = _ffn(xt, gains[0, 2], w_in, w_out, 0, 1, relayout="tb_to_bt").reshape(rows, D_MODEL)

    w_rope = w_dkv[:, KV_LORA:]
    wd = jnp.concatenate([w_dkv[:, :KV_LORA], _pad_cols(w_rope, LANES),
                          _pad_cols(_rot_cols(w_rope), LANES)], axis=-1).astype(BF16)
    w_ukv_h = w_ukv.reshape(KV_LORA, MLA_HEADS, NOPE_DIM + V_DIM)
    wuk = w_ukv_h[:, :, :NOPE_DIM].reshape(KV_LORA, MLA_HEADS * NOPE_DIM).astype(BF16)
    wuv = w_ukv_h[:, :, NOPE_DIM:].reshape(KV_LORA, MLA_HEADS * V_DIM).astype(BF16)
    k, v = _kv(xb, cos_p, sin_p, kv_in_norm.astype(F32).reshape(1, D_MODEL), wd,
               kv_latent_norm.astype(F32).reshape(1, KV_LORA), wuk, wuv)

    xb = _ffn(xb, gains[1, 0], w_in, w_out, 1, 0)
    w_uq_h = mla_w_uq[0].reshape(Q_LORA, MLA_HEADS, NOPE_DIM + ROPE_DIM)
    wq = _pad_cols(w_uq_h, HEAD_PAD).reshape(Q_LORA, MLA_HEADS * HEAD_PAD).astype(BF16)
    wrot = _pad_cols(_rot_cols(w_uq_h[:, :, NOPE_DIM:]), LANES)
    wrot = wrot.reshape(Q_LORA, MLA_HEADS * LANES).astype(BF16)
    q = _q(xb, cos_p, sin_p, gains[1, 1], mla_w_dq[0].astype(BF16),
           mla_q_norm[0].astype(F32).reshape(1, Q_LORA), wq, wrot)
    o = _attention(q.reshape(BATCH, SEQ, -1), k.reshape(BATCH, SEQ, -1),
                   v.reshape(BATCH, SEQ, -1))
    out = _ffn(xb, gains[1, 2], w_in, w_out, 1, 1,
               attn=(o.reshape(rows, MLA_HEADS * V_DIM), mla_w_o[0].astype(BF16)),
               final_gain=final_norm.astype(F32).reshape(1, D_MODEL))
    return out.reshape(BATCH, SEQ, D_MODEL).astype(x.dtype)
```

```python
import functools
import math

import jax
import jax.numpy as jnp
from jax import lax
from jax.experimental import pallas as pl
from jax.experimental.pallas import tpu as pltpu

D_MODEL = 1024
BATCH = 8
SEQ = 2048
CHUNK = 64
EPS = 1e-6
D_FF = 2816
S5_GROUP = 16
S5_GROUPS = D_MODEL // S5_GROUP
S5_STATE = 64
MLA_HEADS = 8
NOPE_DIM = 128
ROPE_DIM = 64
V_DIM = 128
Q_LORA = 256
KV_LORA = 128
ROPE_BASE = 10000.0

LANES = 128
SUBLANES = 8
MXU_DIM = 256
VMEM_LIMIT_BYTES = 56 * 1024 * 1024

HEAD_PAD = 2 * LANES
FFN_STEPS = 64
FFN_ROWS = FFN_STEPS * BATCH
FFN_CHUNKS = ((0, 1024), (1024, 2048), (2048, 2816))
S5_STEPS = 64
S5_ROWS = S5_STEPS * BATCH
S5_KT = D_MODEL // MXU_DIM
S5_GPT = MXU_DIM // S5_GROUP
S5_MODES = S5_GPT * S5_STATE
S5_HALF = S5_MODES // 2
ATT_Q = 256

BF16 = jnp.bfloat16
F32 = jnp.float32


def _rms(x, gain):
    return x * lax.rsqrt(jnp.mean(x * x, axis=-1, keepdims=True) + EPS) * gain


def _resident(shape):
    zeros = (0,) * len(shape)
    return pl.BlockSpec(shape, lambda *_: zeros, pipeline_mode=pl.Buffered(1))


def _params(n_grid_dims):
    return pltpu.CompilerParams(
        dimension_semantics=("arbitrary",) * n_grid_dims,
        vmem_limit_bytes=VMEM_LIMIT_BYTES)


_TAIL_INPUTS = {None: 0, "final": 1, "kv": 3, "q": 3}


def _ffn_kernel(*refs, relayout, with_attn, tail):
    refs = list(refs)
    x_ref = refs.pop(0)
    if with_attn:
        o_ref, wo_ref = refs.pop(0), refs.pop(0)
    g_ref, win_ref, wout_ref = refs.pop(0), refs.pop(0), refs.pop(0)
    tail_refs = [refs.pop(0) for _ in range(_TAIL_INPUTS[tail])]
    out_ref = refs.pop(0)

    x = x_ref[...]
    if relayout == "bt_to_tb":
        x = jnp.swapaxes(x, 0, 1).reshape(FFN_ROWS, D_MODEL)
    elif relayout == "tb_to_bt":
        x = jnp.swapaxes(x.reshape(FFN_STEPS, BATCH, D_MODEL), 0, 1).reshape(FFN_ROWS, D_MODEL)
    if with_attn:
        x = x + jnp.dot(o_ref[...], wo_ref[...], preferred_element_type=F32)
    h = _rms(x, g_ref[...]).astype(BF16)
    acc = None
    for c0, c1 in FFN_CHUNKS:
        gate = jnp.dot(h, win_ref[:, c0:c1], preferred_element_type=F32)
        up = jnp.dot(h, win_ref[:, D_FF + c0:D_FF + c1], preferred_element_type=F32)
        a = (jax.nn.silu(gate) * up).astype(BF16)
        part = jnp.dot(a, wout_ref[c0:c1, :], preferred_element_type=F32)
        acc = part if acc is None else acc + part
    y = x + 0.5 * acc
    if tail == "final":
        (gf_ref,) = tail_refs
        y = _rms(y, gf_ref[...])
    elif tail == "kv":
        gin_ref, wd_ref, glat_ref = tail_refs
        ckv_ref, kr_ref = refs
        kv_a = jnp.dot(_rms(y, gin_ref[...]).astype(BF16), wd_ref[...], preferred_element_type=F32)
        ckv_ref[...] = _rms(kv_a[:, :KV_LORA], glat_ref[...]).astype(BF16).reshape(ckv_ref.shape)
        kr_ref[...] = kv_a[:, KV_LORA:].reshape(kr_ref.shape)
    elif tail == "q":
        gm_ref, wdq_ref, gq_ref = tail_refs
        (cq_ref,) = refs
        c_q = jnp.dot(_rms(y, gm_ref[...]).astype(BF16), wdq_ref[...], preferred_element_type=F32)
        cq_ref[...] = _rms(c_q, gq_ref[...]).astype(BF16)
    out_ref[...] = y.reshape(out_ref.shape)


def _ffn(x, gain, w_in, w_out, layer, half, relayout=None, attn=None, tail=None, tail_args=()):
    rows = BATCH * SEQ
    row_spec = pl.BlockSpec((FFN_ROWS, D_MODEL), lambda i: (i, 0))
    cube_spec = pl.BlockSpec((BATCH, FFN_STEPS, D_MODEL), lambda i: (0, i, 0))
    flat = jax.ShapeDtypeStruct((rows, D_MODEL), F32)
    cube = jax.ShapeDtypeStruct((BATCH, SEQ, D_MODEL), F32)
    in_spec = cube_spec if relayout == "bt_to_tb" else row_spec
    out_spec, out_shape = (cube_spec, cube) if relayout == "tb_to_bt" else (row_spec, flat)

    def weight(w):
        return pl.BlockSpec((None, None) + w.shape[2:], lambda i: (layer, half, 0, 0),
                            pipeline_mode=pl.Buffered(1))

    args, specs = [x], [in_spec]
    if attn is not None:
        o, w_o = attn
        args += [o, w_o]
        specs += [row_spec, _resident(w_o.shape)]
    args += [gain, w_in, w_out]
    specs += [_resident(gain.shape), weight(w_in), weight(w_out)]
    assert len(tail_args) == _TAIL_INPUTS[tail]
    args += list(tail_args)
    specs += [_resident(a.shape) for a in tail_args]
    out_specs, out_shapes = [out_spec], [out_shape]
    if tail == "kv":
        assert relayout == "tb_to_bt"
        out_specs += [pl.BlockSpec((BATCH, FFN_STEPS, LANES), lambda i: (0, i, 0))] * 2
        out_shapes += [jax.ShapeDtypeStruct((BATCH, SEQ, KV_LORA), BF16),
                       jax.ShapeDtypeStruct((BATCH, SEQ, LANES), F32)]
    elif tail == "q":
        out_specs.append(pl.BlockSpec((FFN_ROWS, Q_LORA), lambda i: (i, 0)))
        out_shapes.append(jax.ShapeDtypeStruct((rows, Q_LORA), BF16))
    outs = pl.pallas_call(
        functools.partial(_ffn_kernel, relayout=relayout, with_attn=attn is not None, tail=tail),
        grid=(rows // FFN_ROWS,),
        in_specs=specs,
        out_specs=out_specs,
        out_shape=out_shapes,
        compiler_params=_params(1),
        name="ffn",
    )(*args)
    return outs[0] if len(outs) == 1 else outs


def _s5_kernel(x_ref, g_ref, wb_ref, lam_ref, wc_ref, d_ref, wglu_ref, out_ref,
               bu_ref, state_ref):
    @pl.when(pl.program_id(0) == 0)
    def _():
        state_ref[...] = jnp.zeros_like(state_ref)

    x = x_ref[...]
    h = _rms(x, g_ref[...])
    hb = h.astype(BF16)

    def project_in(kt):
        bu_ref[kt] = jnp.dot(hb[:, kt * MXU_DIM:(kt + 1) * MXU_DIM], wb_ref[kt],
                             preferred_element_type=F32)

    def scan(kt):
        cols = [(half * S5_HALF, S5_MODES + half * S5_HALF) for half in range(2)]
        state = [(state_ref[kt, :, re0:re0 + S5_HALF], state_ref[kt, :, im0:im0 + S5_HALF])
                 for re0, im0 in cols]
        for t in range(S5_STEPS):
            r0 = t * SUBLANES
            for half, (re0, im0) in enumerate(cols):
                lam_re = lam_ref[kt, :, re0:re0 + S5_HALF]
                lam_im = lam_ref[kt, :, im0:im0 + S5_HALF]
                s_re, s_im = state[half]
                n_re = lam_re * s_re - lam_im * s_im + bu_ref[kt, r0:r0 + SUBLANES, re0:re0 + S5_HALF]
                n_im = lam_re * s_im + lam_im * s_re + bu_ref[kt, r0:r0 + SUBLANES, im0:im0 + S5_HALF]
                bu_ref[kt, r0:r0 + SUBLANES, re0:re0 + S5_HALF] = n_re
                bu_ref[kt, r0:r0 + SUBLANES, im0:im0 + S5_HALF] = n_im
                state[half] = (n_re, n_im)
        for (re0, im0), (s_re, s_im) in zip(cols, state):
            state_ref[kt, :, re0:re0 + S5_HALF] = s_re
            state_ref[kt, :, im0:im0 + S5_HALF] = s_im

    ys = []
    project_in(0)
    for kt in range(S5_KT):
        if kt + 1 < S5_KT:
            project_in(kt + 1)
        scan(kt)
        ys.append(jnp.dot(bu_ref[kt].astype(BF16), wc_ref[kt], preferred_element_type=F32))
    y = jnp.concatenate(ys, axis=1) + d_ref[...] * h
    y = jax.nn.gelu(y, approximate=True).astype(BF16)
    z = jnp.dot(y, wglu_ref[...], preferred_element_type=F32)
    out_ref[...] = x + z[:, :D_MODEL] * jax.nn.sigmoid(z[:, D_MODEL:])


def _s5(x, gain, wb, lam, wc, d_skip, w_glu):
    rows = x.shape[0]
    row_spec = pl.BlockSpec((S5_ROWS, D_MODEL), lambda i: (i, 0))
    return pl.pallas_call(
        _s5_kernel,
        grid=(rows // S5_ROWS,),
        in_specs=[row_spec, _resident(gain.shape), _resident(wb.shape), _resident(lam.shape),
                  _resident(wc.shape), _resident(d_skip.shape), _resident(w_glu.shape)],
        out_specs=row_spec,
        out_shape=jax.ShapeDtypeStruct((rows, D_MODEL), F32),
        scratch_shapes=[pltpu.VMEM((S5_KT, S5_ROWS, 2 * S5_MODES), F32),
                        pltpu.VMEM((S5_KT, SUBLANES, 2 * S5_MODES), F32)],
        compiler_params=_params(1),
        name="s5_mixer",
    )(x, gain, wb, lam, wc, d_skip, w_glu)


def _s5_disc_kernel(lam_re_ref, lam_im_ref, lam_re_rep_ref, lam_im_rep_ref, log_step_ref,
                    b_re_ref, b_im_ref, lbar_re_ref, lbar_im_ref, bbar_re_ref, bbar_im_ref):
    step = jnp.exp(log_step_ref[...])

    def lam_bar(lam_re, lam_im):
        mag = jnp.exp(lam_re * step)
        return mag * jnp.cos(lam_im * step), mag * jnp.sin(lam_im * step)

    lbar_re_ref[...], lbar_im_ref[...] = lam_bar(lam_re_ref[...], lam_im_ref[...])
    lam_re, lam_im = lam_re_rep_ref[...], lam_im_rep_ref[...]
    lb_re, lb_im = lam_bar(lam_re, lam_im)
    n_re, n_im = lb_re - 1.0, lb_im
    den = lam_re * lam_re + lam_im * lam_im
    f_re = (n_re * lam_re + n_im * lam_im) / den
    f_im = (n_im * lam_re - n_re * lam_im) / den
    b_re, b_im = b_re_ref[...], b_im_ref[...]
    bbar_re_ref[...] = f_re * b_re - f_im * b_im
    bbar_im_ref[...] = f_re * b_im + f_im * b_re


def _s5_weights(lam_re, lam_im, log_step, b_re, b_im, c_re, c_im):
    pc = S5_STATE * S5_GROUP
    gp = jax.ShapeDtypeStruct((S5_GROUPS, S5_STATE), F32)
    gpc = jax.ShapeDtypeStruct((S5_GROUPS, pc), F32)
    lbar_re, lbar_im, bbar_re, bbar_im = pl.pallas_call(
        _s5_disc_kernel, out_shape=[gp, gp, gpc, gpc], name="s5_discretise",
    )(lam_re.astype(F32), lam_im.astype(F32),
      jnp.repeat(lam_re.astype(F32), S5_GROUP, axis=1), jnp.repeat(lam_im.astype(F32), S5_GROUP, axis=1),
      log_step.astype(F32).reshape(S5_GROUPS, 1),
      b_re.astype(F32).reshape(S5_GROUPS, pc), b_im.astype(F32).reshape(S5_GROUPS, pc))
    eye = jnp.eye(S5_GPT, dtype=F32)

    def pack_b(part):
        part = part.reshape(S5_KT, S5_GPT, S5_STATE, S5_GROUP)
        return jnp.einsum('kgpc,gh->kgchp', part, eye).reshape(S5_KT, MXU_DIM, S5_MODES)

    def pack_c(part):
        part = part.reshape(S5_KT, S5_GPT, S5_GROUP, S5_STATE)
        return jnp.einsum('kgcp,gh->kgphc', part, eye).reshape(S5_KT, S5_MODES, MXU_DIM)

    wb = jnp.concatenate([pack_b(bbar_re), pack_b(bbar_im)], axis=-1)
    wc = jnp.concatenate([pack_c(c_re.astype(F32)), -pack_c(c_im.astype(F32))], axis=1)
    lam_row = jnp.concatenate([lbar_re.reshape(S5_KT, S5_MODES),
                               lbar_im.reshape(S5_KT, S5_MODES)], axis=-1)
    lam_tile = jnp.broadcast_to(lam_row[:, None, :], (S5_KT, SUBLANES, 2 * S5_MODES))
    return wb.astype(BF16), lam_tile, wc.astype(BF16)


def _attn_kernel(cq_ref, ckv_ref, kr_ref, rot_ref, wq_ref, wkv_ref, o_ref):
    scale = math.log2(math.e) / math.sqrt(NOPE_DIM + ROPE_DIM)
    rot = rot_ref[...]
    low_lanes = lax.broadcasted_iota(jnp.int32, (SEQ, LANES), 1) < ROPE_DIM

    def rope(parts):
        t = parts * rot
        return jnp.where(low_lanes, t + pltpu.roll(t, ROPE_DIM, axis=1), 0.0)

    q = jnp.dot(cq_ref[...], wq_ref[...], preferred_element_type=F32)
    q = jnp.concatenate([(q[:, :NOPE_DIM] * scale).astype(BF16),
                         (rope(q[:, NOPE_DIM:]) * scale).astype(BF16)], axis=1)
    kv = jnp.dot(ckv_ref[...], wkv_ref[...], preferred_element_type=F32)
    k = jnp.concatenate([kv[:, :NOPE_DIM].astype(BF16), rope(kr_ref[...]).astype(BF16)], axis=1)
    v_ones = jnp.concatenate([kv[:, NOPE_DIM:].astype(BF16), jnp.ones((SEQ, V_DIM), BF16)], axis=1)

    row_chunk = lax.broadcasted_iota(jnp.int32, (ATT_Q, ATT_Q), 0) // CHUNK
    col_chunk = lax.broadcasted_iota(jnp.int32, (ATT_Q, ATT_Q), 1) // CHUNK
    diag_mask = col_chunk <= row_chunk
    neg = jnp.finfo(F32).min
    for i in reversed(range(SEQ // ATT_Q)):
        q0, n_keys = i * ATT_Q, (i + 1) * ATT_Q
        s = lax.dot_general(q[q0:q0 + ATT_Q, :], k[0:n_keys, :],
                            (((1,), (1,)), ((), ())), preferred_element_type=F32)
        s_diag = jnp.where(diag_mask, s[:, q0:n_keys], neg)
        s = s_diag if i == 0 else jnp.concatenate([s[:, :q0], s_diag], axis=1)
        m = jnp.max(s, axis=-1, keepdims=True)
        p = jnp.exp2(s - m).astype(BF16)
        o = jnp.dot(p, v_ones[0:n_keys, :], preferred_element_type=F32)
        o_ref[q0:q0 + ATT_Q, :] = (o[:, :V_DIM] / o[:, V_DIM:]).astype(BF16)


def _attention(c_q, c_kv, k_parts, rot_table, wq, wkv):
    def per_batch(width):
        return pl.BlockSpec((None, SEQ, width), lambda b, h: (b, 0, 0))

    def per_head(w):
        return pl.BlockSpec((None,) + w.shape[1:], lambda b, h: (h, 0, 0))

    return pl.pallas_call(
        _attn_kernel,
        grid=(BATCH, MLA_HEADS),
        in_specs=[per_batch(Q_LORA), per_batch(KV_LORA), per_batch(LANES), per_batch(LANES),
                  per_head(wq), per_head(wkv)],
        out_specs=pl.BlockSpec((None, SEQ, V_DIM), lambda b, h: (b, 0, h)),
        out_shape=jax.ShapeDtypeStruct((BATCH, SEQ, MLA_HEADS * V_DIM), BF16),
        compiler_params=_params(2),
        name="mla_attention",
    )(c_q, c_kv, k_parts, rot_table, wq, wkv)


def _rope_table(positions):
    inv_freq = 1.0 / (ROPE_BASE ** (jnp.arange(0, ROPE_DIM, 2, dtype=F32) / ROPE_DIM))
    ang = positions.astype(F32)[..., None] * inv_freq
    cos, sin = jnp.cos(ang), jnp.sin(ang)
    return jnp.concatenate([cos, cos, -sin, sin], axis=-1)


def _with_rot_cols(w):
    half = ROPE_DIM // 2
    return jnp.concatenate([w, w[..., half:], w[..., :half]], axis=-1)


def kernel(x, positions, norm_gains, ffn_w_in, ffn_w_out, s5_lambda_re, s5_lambda_im, s5_log_step, s5_b_re, s5_b_im, s5_c_re, s5_c_im, s5_d, s5_w_glu, kv_in_norm, w_dkv, kv_latent_norm, w_ukv, mla_w_dq, mla_q_norm, mla_w_uq, mla_w_o, final_norm):
    assert x.shape == (BATCH, SEQ, D_MODEL) and norm_gains.shape[0] == 2
    rows = BATCH * SEQ
    gains = norm_gains.astype(F32).reshape(2, 3, 1, D_MODEL)
    w_in = ffn_w_in.astype(BF16)
    w_out = ffn_w_out.astype(BF16)

    xt = _ffn(x.astype(F32), gains[0, 0], w_in, w_out, 0, 0, relayout="bt_to_tb")
    wb, lam_tile, wc = _s5_weights(s5_lambda_re[0], s5_lambda_im[0], s5_log_step[0],
                                   s5_b_re[0], s5_b_im[0], s5_c_re[0], s5_c_im[0])
    xt = _s5(xt, gains[0, 1], wb, lam_tile, wc, s5_d[0].astype(F32).reshape(1, D_MODEL),
             s5_w_glu[0].astype(BF16))
    wd = jnp.concatenate([w_dkv[:, :KV_LORA], _with_rot_cols(w_dkv[:, KV_LORA:])], axis=-1)
    xb, c_kv, k_parts = _ffn(
        xt, gains[0, 2], w_in, w_out, 0, 1, relayout="tb_to_bt", tail="kv",
        tail_args=(kv_in_norm.astype(F32).reshape(1, D_MODEL), wd.astype(BF16),
                   kv_latent_norm.astype(F32).reshape(1, KV_LORA)))
    xb = xb.reshape(rows, D_MODEL)

    xb, c_q = _ffn(xb, gains[1, 0], w_in, w_out, 1, 0, tail="q",
                   tail_args=(gains[1, 1], mla_w_dq[0].astype(BF16),
                              mla_q_norm[0].astype(F32).reshape(1, Q_LORA)))
    w_uq_h = mla_w_uq[0].reshape(Q_LORA, MLA_HEADS, NOPE_DIM + ROPE_DIM)
    wq = jnp.concatenate([w_uq_h[..., :NOPE_DIM], _with_rot_cols(w_uq_h[..., NOPE_DIM:])], axis=-1)
    wq = wq.transpose(1, 0, 2).astype(BF16)
    wkv = w_ukv.reshape(KV_LORA, MLA_HEADS, NOPE_DIM + V_DIM).transpose(1, 0, 2).astype(BF16)
    o = _attention(c_q.reshape(BATCH, SEQ, Q_LORA), c_kv, k_parts, _rope_table(positions), wq, wkv)
    out = _ffn(xb, gains[1, 2], w_in, w_out, 1, 1,
               attn=(o.reshape(rows, MLA_HEADS * V_DIM), mla_w_o[0].astype(BF16)),
               tail="final", tail_args=(final_norm.astype(F32).reshape(1, D_MODEL),))
    return out.reshape(BATCH, SEQ, D_MODEL).astype(x.dtype)
```

```python
import functools
import math

import jax
import jax.numpy as jnp
from jax import lax
from jax.experimental import pallas as pl
from jax.experimental.pallas import tpu as pltpu

D_MODEL = 1024
BATCH = 8
SEQ = 2048
CHUNK = 64
EPS = 1e-6
D_FF = 2816
S5_GROUP = 16
S5_GROUPS = D_MODEL // S5_GROUP
S5_STATE = 64
MLA_HEADS = 8
NOPE_DIM = 128
ROPE_DIM = 64
V_DIM = 128
Q_LORA = 256
KV_LORA = 128
ROPE_BASE = 10000.0

LANES = 128
SUBLANES = 8
MXU_DIM = 256
VMEM_LIMIT_BYTES = 56 * 1024 * 1024

HEAD_PAD = 2 * LANES
FFN_STEPS = 128
FFN_ROWS = FFN_STEPS * BATCH
FFN_SUBTILES = 2
SUB_STEPS = FFN_STEPS // FFN_SUBTILES
SUB_ROWS = FFN_ROWS // FFN_SUBTILES
FFN_CHUNKS = ((0, 1024), (1024, 2048), (2048, 2816))
S5_STEPS = 64
S5_ROWS = S5_STEPS * BATCH
S5_KT = D_MODEL // MXU_DIM
S5_GPT = MXU_DIM // S5_GROUP
S5_MODES = S5_GPT * S5_STATE
S5_HALF = S5_MODES // 2
ATT_Q = 256

BF16 = jnp.bfloat16
F32 = jnp.float32


def _rms(x, gain):
    return x * lax.rsqrt(jnp.mean(x * x, axis=-1, keepdims=True) + EPS) * gain


def _resident(shape):
    zeros = (0,) * len(shape)
    return pl.BlockSpec(shape, lambda *_: zeros, pipeline_mode=pl.Buffered(1))


def _params(n_grid_dims):
    return pltpu.CompilerParams(
        dimension_semantics=("arbitrary",) * n_grid_dims,
        vmem_limit_bytes=VMEM_LIMIT_BYTES)


_TAIL_INPUTS = {None: 0, "final": 1, "kv": 3, "q": 3}


def _ffn_kernel(*refs, relayout, with_attn, tail):
    refs = list(refs)
    x_ref = refs.pop(0)
    if with_attn:
        o_ref, wo_ref = refs.pop(0), refs.pop(0)
    g_ref, win_ref, wout_ref = refs.pop(0), refs.pop(0), refs.pop(0)
    tail_refs = [refs.pop(0) for _ in range(_TAIL_INPUTS[tail])]
    out_ref = refs.pop(0)

    for sub in range(FFN_SUBTILES):
        t0, r0 = sub * SUB_STEPS, sub * SUB_ROWS
        steps, rws = slice(t0, t0 + SUB_STEPS), slice(r0, r0 + SUB_ROWS)
        if relayout == "bt_to_tb":
            x = jnp.swapaxes(x_ref[:, steps, :], 0, 1).reshape(SUB_ROWS, D_MODEL)
        elif relayout == "tb_to_bt":
            x = jnp.swapaxes(x_ref[rws, :].reshape(SUB_STEPS, BATCH, D_MODEL), 0, 1)
            x = x.reshape(SUB_ROWS, D_MODEL)
        else:
            x = x_ref[rws, :]
        if with_attn:
            x = x + jnp.dot(o_ref[rws, :], wo_ref[...], preferred_element_type=F32)
        h = _rms(x, g_ref[...]).astype(BF16)
        acc = None
        for c0, c1 in FFN_CHUNKS:
            gate = jnp.dot(h, win_ref[:, c0:c1], preferred_element_type=F32)
            up = jnp.dot(h, win_ref[:, D_FF + c0:D_FF + c1], preferred_element_type=F32)
            a = (jax.nn.silu(gate) * up).astype(BF16)
            part = jnp.dot(a, wout_ref[c0:c1, :], preferred_element_type=F32)
            acc = part if acc is None else acc + part
        y = x + 0.5 * acc
        cube = (BATCH, SUB_STEPS, -1)
        if tail == "final":
            (gf_ref,) = tail_refs
            y = _rms(y, gf_ref[...])
        elif tail == "kv":
            gin_ref, wd_ref, glat_ref = tail_refs
            ckv_ref, kr_ref = refs
            kv_a = jnp.dot(_rms(y, gin_ref[...]).astype(BF16), wd_ref[...], preferred_element_type=F32)
            ckv_ref[:, steps, :] = _rms(kv_a[:, :KV_LORA], glat_ref[...]).astype(BF16).reshape(cube)
            kr_ref[:, steps, :] = kv_a[:, KV_LORA:].reshape(cube)
        elif tail == "q":
            gm_ref, wdq_ref, gq_ref = tail_refs
            (cq_ref,) = refs
            c_q = jnp.dot(_rms(y, gm_ref[...]).astype(BF16), wdq_ref[...], preferred_element_type=F32)
            cq_ref[rws, :] = _rms(c_q, gq_ref[...]).astype(BF16)
        if relayout == "tb_to_bt":
            out_ref[:, steps, :] = y.reshape(cube)
        else:
            out_ref[rws, :] = y


def _ffn(x, gain, w_in, w_out, layer, half, relayout=None, attn=None, tail=None, tail_args=()):
    rows = BATCH * SEQ
    row_spec = pl.BlockSpec((FFN_ROWS, D_MODEL), lambda i: (i, 0))
    cube_spec = pl.BlockSpec((BATCH, FFN_STEPS, D_MODEL), lambda i: (0, i, 0))
    flat = jax.ShapeDtypeStruct((rows, D_MODEL), F32)
    cube = jax.ShapeDtypeStruct((BATCH, SEQ, D_MODEL), F32)
    in_spec = cube_spec if relayout == "bt_to_tb" else row_spec
    out_spec, out_shape = (cube_spec, cube) if relayout == "tb_to_bt" else (row_spec, flat)

    def weight(w):
        return pl.BlockSpec((None, None) + w.shape[2:], lambda i: (layer, half, 0, 0),
                            pipeline_mode=pl.Buffered(1))

    args, specs = [x], [in_spec]
    if attn is not None:
        o, w_o = attn
        args += [o, w_o]
        specs += [row_spec, _resident(w_o.shape)]
    args += [gain, w_in, w_out]
    specs += [_resident(gain.shape), weight(w_in), weight(w_out)]
    assert len(tail_args) == _TAIL_INPUTS[tail]
    args += list(tail_args)
    specs += [_resident(a.shape) for a in tail_args]
    out_specs, out_shapes = [out_spec], [out_shape]
    if tail == "kv":
        assert relayout == "tb_to_bt"
        out_specs += [pl.BlockSpec((BATCH, FFN_STEPS, LANES), lambda i: (0, i, 0))] * 2
        out_shapes += [jax.ShapeDtypeStruct((BATCH, SEQ, KV_LORA), BF16),
                       jax.ShapeDtypeStruct((BATCH, SEQ, LANES), F32)]
    elif tail == "q":
        out_specs.append(pl.BlockSpec((FFN_ROWS, Q_LORA), lambda i: (i, 0)))
        out_shapes.append(jax.ShapeDtypeStruct((rows, Q_LORA), BF16))
    outs = pl.pallas_call(
        functools.partial(_ffn_kernel, relayout=relayout, with_attn=attn is not None, tail=tail),
        grid=(rows // FFN_ROWS,),
        in_specs=specs,
        out_specs=out_specs,
        out_shape=out_shapes,
        compiler_params=_params(1),
        name="ffn",
    )(*args)
    return outs[0] if len(outs) == 1 else outs


def _s5_kernel(x_ref, g_ref, wb_ref, lam_ref, wc_ref, d_ref, wglu_ref, out_ref,
               bu_ref, state_ref):
    @pl.when(pl.program_id(0) == 0)
    def _():
        state_ref[...] = jnp.zeros_like(state_ref)

    x = x_ref[...]
    h = _rms(x, g_ref[...])
    hb = h.astype(BF16)

    def project_in(kt):
        bu_ref[kt] = jnp.dot(hb[:, kt * MXU_DIM:(kt + 1) * MXU_DIM], wb_ref[kt],
                             preferred_element_type=F32)

    def scan(kt):
        cols = [(half * S5_HALF, S5_MODES + half * S5_HALF) for half in range(2)]
        state = [(state_ref[kt, :, re0:re0 + S5_HALF], state_ref[kt, :, im0:im0 + S5_HALF])
                 for re0, im0 in cols]
        for t in range(S5_STEPS):
            r0 = t * SUBLANES
            for half, (re0, im0) in enumerate(cols):
                lam_re = lam_ref[kt, :, re0:re0 + S5_HALF]
                lam_im = lam_ref[kt, :, im0:im0 + S5_HALF]
                s_re, s_im = state[half]
                n_re = lam_re * s_re - lam_im * s_im + bu_ref[kt, r0:r0 + SUBLANES, re0:re0 + S5_HALF]
                n_im = lam_re * s_im + lam_im * s_re + bu_ref[kt, r0:r0 + SUBLANES, im0:im0 + S5_HALF]
                bu_ref[kt, r0:r0 + SUBLANES, re0:re0 + S5_HALF] = n_re
                bu_ref[kt, r0:r0 + SUBLANES, im0:im0 + S5_HALF] = n_im
                state[half] = (n_re, n_im)
        for (re0, im0), (s_re, s_im) in zip(cols, state):
            state_ref[kt, :, re0:re0 + S5_HALF] = s_re
            state_ref[kt, :, im0:im0 + S5_HALF] = s_im

    ys = []
    project_in(0)
    for kt in range(S5_KT):
        if kt + 1 < S5_KT:
            project_in(kt + 1)
        scan(kt)
        ys.append(jnp.dot(bu_ref[kt].astype(BF16), wc_ref[kt], preferred_element_type=F32))
    y = jnp.concatenate(ys, axis=1) + d_ref[...] * h
    y = jax.nn.gelu(y, approximate=True).astype(BF16)
    z = jnp.dot(y, wglu_ref[...], preferred_element_type=F32)
    out_ref[...] = x + z[:, :D_MODEL] * jax.nn.sigmoid(z[:, D_MODEL:])


def _s5(x, gain, wb, lam, wc, d_skip, w_glu):
    rows = x.shape[0]
    row_spec = pl.BlockSpec((S5_ROWS, D_MODEL), lambda i: (i, 0))
    return pl.pallas_call(
        _s5_kernel,
        grid=(rows // S5_ROWS,),
        in_specs=[row_spec, _resident(gain.shape), _resident(wb.shape), _resident(lam.shape),
                  _resident(wc.shape), _resident(d_skip.shape), _resident(w_glu.shape)],
        out_specs=row_spec,
        out_shape=jax.ShapeDtypeStruct((rows, D_MODEL), F32),
        scratch_shapes=[pltpu.VMEM((S5_KT, S5_ROWS, 2 * S5_MODES), F32),
                        pltpu.VMEM((S5_KT, SUBLANES, 2 * S5_MODES), F32)],
        compiler_params=_params(1),
        name="s5_mixer",
    )(x, gain, wb, lam, wc, d_skip, w_glu)


def _s5_disc_kernel(lam_re_ref, lam_im_ref, lam_re_rep_ref, lam_im_rep_ref, log_step_ref,
                    b_re_ref, b_im_ref, lbar_re_ref, lbar_im_ref, bbar_re_ref, bbar_im_ref):
    step = jnp.exp(log_step_ref[...])

    def lam_bar(lam_re, lam_im):
        mag = jnp.exp(lam_re * step)
        return mag * jnp.cos(lam_im * step), mag * jnp.sin(lam_im * step)

    lbar_re_ref[...], lbar_im_ref[...] = lam_bar(lam_re_ref[...], lam_im_ref[...])
    lam_re, lam_im = lam_re_rep_ref[...], lam_im_rep_ref[...]
    lb_re, lb_im = lam_bar(lam_re, lam_im)
    n_re, n_im = lb_re - 1.0, lb_im
    den = lam_re * lam_re + lam_im * lam_im
    f_re = (n_re * lam_re + n_im * lam_im) / den
    f_im = (n_im * lam_re - n_re * lam_im) / den
    b_re, b_im = b_re_ref[...], b_im_ref[...]
    bbar_re_ref[...] = f_re * b_re - f_im * b_im
    bbar_im_ref[...] = f_re * b_im + f_im * b_re


def _s5_weights(lam_re, lam_im, log_step, b_re, b_im, c_re, c_im):
    pc = S5_STATE * S5_GROUP
    gp = jax.ShapeDtypeStruct((S5_GROUPS, S5_STATE), F32)
    gpc = jax.ShapeDtypeStruct((S5_GROUPS, pc), F32)
    lbar_re, lbar_im, bbar_re, bbar_im = pl.pallas_call(
        _s5_disc_kernel, out_shape=[gp, gp, gpc, gpc], name="s5_discretise",
    )(lam_re.astype(F32), lam_im.astype(F32),
      jnp.repeat(lam_re.astype(F32), S5_GROUP, axis=1), jnp.repeat(lam_im.astype(F32), S5_GROUP, axis=1),
      log_step.astype(F32).reshape(S5_GROUPS, 1),
      b_re.astype(F32).reshape(S5_GROUPS, pc), b_im.astype(F32).reshape(S5_GROUPS, pc))
    eye = jnp.eye(S5_GPT, dtype=F32)

    def pack_b(part):
        part = part.reshape(S5_KT, S5_GPT, S5_STATE, S5_GROUP)
        return jnp.einsum('kgpc,gh->kgchp', part, eye).reshape(S5_KT, MXU_DIM, S5_MODES)

    def pack_c(part):
        part = part.reshape(S5_KT, S5_GPT, S5_GROUP, S5_STATE)
        return jnp.einsum('kgcp,gh->kgphc', part, eye).reshape(S5_KT, S5_MODES, MXU_DIM)

    wb = jnp.concatenate([pack_b(bbar_re), pack_b(bbar_im)], axis=-1)
    wc = jnp.concatenate([pack_c(c_re.astype(F32)), -pack_c(c_im.astype(F32))], axis=1)
    lam_row = jnp.concatenate([lbar_re.reshape(S5_KT, S5_MODES),
                               lbar_im.reshape(S5_KT, S5_MODES)], axis=-1)
    lam_tile = jnp.broadcast_to(lam_row[:, None, :], (S5_KT, SUBLANES, 2 * S5_MODES))
    return wb.astype(BF16), lam_tile, wc.astype(BF16)


def _attn_kernel(cq_ref, ckv_ref, kr_ref, rot_ref, wq_ref, wkv_ref, o_ref):
    scale = math.log2(math.e) / math.sqrt(NOPE_DIM + ROPE_DIM)
    rot = rot_ref[...]
    low_lanes = lax.broadcasted_iota(jnp.int32, (SEQ, LANES), 1) < ROPE_DIM

    def rope(parts):
        t = parts * rot
        return jnp.where(low_lanes, t + pltpu.roll(t, ROPE_DIM, axis=1), 0.0)

    q = jnp.dot(cq_ref[...], wq_ref[...], preferred_element_type=F32)
    q = jnp.concatenate([(q[:, :NOPE_DIM] * scale).astype(BF16),
                         (rope(q[:, NOPE_DIM:]) * scale).astype(BF16)], axis=1)
    kv = jnp.dot(ckv_ref[...], wkv_ref[...], preferred_element_type=F32)
    k = jnp.concatenate([kv[:, :NOPE_DIM].astype(BF16), rope(kr_ref[...]).astype(BF16)], axis=1)
    v_ones = jnp.concatenate([kv[:, NOPE_DIM:].astype(BF16), jnp.ones((SEQ, V_DIM), BF16)], axis=1)

    row_chunk = lax.broadcasted_iota(jnp.int32, (ATT_Q, ATT_Q), 0) // CHUNK
    col_chunk = lax.broadcasted_iota(jnp.int32, (ATT_Q, ATT_Q), 1) // CHUNK
    diag_mask = col_chunk <= row_chunk
    neg = jnp.finfo(F32).min
    for i in reversed(range(SEQ // ATT_Q)):
        q0, n_keys = i * ATT_Q, (i + 1) * ATT_Q
        s = lax.dot_general(q[q0:q0 + ATT_Q, :], k[0:n_keys, :],
                            (((1,), (1,)), ((), ())), preferred_element_type=F32)
        s_diag = jnp.where(diag_mask, s[:, q0:n_keys], neg)
        s = s_diag if i == 0 else jnp.concatenate([s[:, :q0], s_diag], axis=1)
        m = jnp.max(s, axis=-1, keepdims=True)
        p = jnp.exp2(s - m).astype(BF16)
        o = jnp.dot(p, v_ones[0:n_keys, :], preferred_element_type=F32)
        o_ref[q0:q0 + ATT_Q, :] = (o[:, :V_DIM] / o[:, V_DIM:]).astype(BF16)


def _attention(c_q, c_kv, k_parts, rot_table, wq, wkv):
    def per_batch(width):
        return pl.BlockSpec((None, SEQ, width), lambda b, h: (b, 0, 0))

    def per_head(w):
        return pl.BlockSpec((None,) + w.shape[1:], lambda b, h: (h, 0, 0))

    return pl.pallas_call(
        _attn_kernel,
        grid=(BATCH, MLA_HEADS),
        in_specs=[per_batch(Q_LORA), per_batch(KV_LORA), per_batch(LANES), per_batch(LANES),
                  per_head(wq), per_head(wkv)],
        out_specs=pl.BlockSpec((None, SEQ, V_DIM), lambda b, h: (b, 0, h)),
        out_shape=jax.ShapeDtypeStruct((BATCH, SEQ, MLA_HEADS * V_DIM), BF16),
        compiler_params=_params(2),
        name="mla_attention",
    )(c_q, c_kv, k_parts, rot_table, wq, wkv)


def _rope_table(positions):
    inv_freq = 1.0 / (ROPE_BASE ** (jnp.arange(0, ROPE_DIM, 2, dtype=F32) / ROPE_DIM))
    ang = positions.astype(F32)[..., None] * inv_freq
    cos, sin = jnp.cos(ang), jnp.sin(ang)
    return jnp.concatenate([cos, cos, -sin, sin], axis=-1)


def _with_rot_cols(w):
    half = ROPE_DIM // 2
    return jnp.concatenate([w, w[..., half:], w[..., :half]], axis=-1)


def kernel(x, positions, norm_gains, ffn_w_in, ffn_w_out, s5_lambda_re, s5_lambda_im, s5_log_step, s5_b_re, s5_b_im, s5_c_re, s5_c_im, s5_d, s5_w_glu, kv_in_norm, w_dkv, kv_latent_norm, w_ukv, mla_w_dq, mla_q_norm, mla_w_uq, mla_w_o, final_norm):
    assert x.shape == (BATCH, SEQ, D_MODEL) and norm_gains.shape[0] == 2
    rows = BATCH * SEQ
    gains = norm_gains.astype(F32).reshape(2, 3, 1, D_MODEL)
    w_in = ffn_w_in.astype(BF16)
    w_out = ffn_w_out.astype(BF16)

    xt = _ffn(x.astype(F32), gains[0, 0], w_in, w_out, 0, 0, relayout="bt_to_tb")
    wb, lam_tile, wc = _s5_weights(s5_lambda_re[0], s5_lambda_im[0], s5_log_step[0],
                                   s5_b_re[0], s5_b_im[0], s5_c_re[0], s5_c_im[0])
    xt = _s5(xt, gains[0, 1], wb, lam_tile, wc, s5_d[0].astype(F32).reshape(1, D_MODEL),
             s5_w_glu[0].astype(BF16))
    wd = jnp.concatenate([w_dkv[:, :KV_LORA], _with_rot_cols(w_dkv[:, KV_LORA:])], axis=-1)
    xb, c_kv, k_parts = _ffn(
        xt, gains[0, 2], w_in, w_out, 0, 1, relayout="tb_to_bt", tail="kv",
        tail_args=(kv_in_norm.astype(F32).reshape(1, D_MODEL), wd.astype(BF16),
                   kv_latent_norm.astype(F32).reshape(1, KV_LORA)))
    xb = xb.reshape(rows, D_MODEL)

    xb, c_q = _ffn(xb, gains[1, 0], w_in, w_out, 1, 0, tail="q",
                   tail_args=(gains[1, 1], mla_w_dq[0].astype(BF16),
                              mla_q_norm[0].astype(F32).reshape(1, Q_LORA)))
    w_uq_h = mla_w_uq[0].reshape(Q_LORA, MLA_HEADS, NOPE_DIM + ROPE_DIM)
    wq = jnp.concatenate([w_uq_h[..., :NOPE_DIM], _with_rot_cols(w_uq_h[..., NOPE_DIM:])], axis=-1)
    wq = wq.transpose(1, 0, 2).astype(BF16)
    wkv = w_ukv.reshape(KV_LORA, MLA_HEADS, NOPE_DIM + V_DIM).transpose(1, 0, 2).astype(BF16)
    o = _attention(c_q.reshape(BATCH, SEQ, Q_LORA), c_kv, k_parts, _rope_table(positions), wq, wkv)
    out = _ffn(xb, gains[1, 2], w_in, w_out, 1, 1,
               attn=(o.reshape(rows, MLA_HEADS * V_DIM), mla_w_o[0].astype(BF16)),
               tail="final", tail_args=(final_norm.astype(F32).reshape(1, D_MODEL),))
    return out.reshape(BATCH, SEQ, D_MODEL).astype(x.dtype)
```

```python
import functools
import math

import jax
import jax.numpy as jnp
from jax import lax
from jax.experimental import pallas as pl
from jax.experimental.pallas import tpu as pltpu

D_MODEL = 1024
BATCH = 8
SEQ = 2048
CHUNK = 64
EPS = 1e-6
D_FF = 2816
S5_GROUP = 16
S5_GROUPS = D_MODEL // S5_GROUP
S5_STATE = 64
MLA_HEADS = 8
NOPE_DIM = 128
ROPE_DIM = 64
V_DIM = 128
Q_LORA = 256
KV_LORA = 128
ROPE_BASE = 10000.0

LANES = 128
SUBLANES = 8
MXU_DIM = 256
VMEM_LIMIT_BYTES = 56 * 1024 * 1024

HEAD_PAD = 2 * LANES
FFN_STEPS = 128
FFN_ROWS = FFN_STEPS * BATCH
FFN_SUBTILES = 2
SUB_STEPS = FFN_STEPS // FFN_SUBTILES
SUB_ROWS = FFN_ROWS // FFN_SUBTILES
FFN_CHUNKS = ((0, 1024), (1024, 2048), (2048, 2816))
S5_STEPS = 64
S5_ROWS = S5_STEPS * BATCH
S5_KT = D_MODEL // MXU_DIM
S5_GPT = MXU_DIM // S5_GROUP
S5_MODES = S5_GPT * S5_STATE
S5_HALF = S5_MODES // 2
ATT_Q = 256
ATT_HEADS = 1

BF16 = jnp.bfloat16
F32 = jnp.float32


def _rms(x, gain):
    return x * lax.rsqrt(jnp.mean(x * x, axis=-1, keepdims=True) + EPS) * gain


def _resident(shape):
    zeros = (0,) * len(shape)
    return pl.BlockSpec(shape, lambda *_: zeros, pipeline_mode=pl.Buffered(1))


def _params(n_grid_dims):
    return pltpu.CompilerParams(
        dimension_semantics=("arbitrary",) * n_grid_dims,
        vmem_limit_bytes=VMEM_LIMIT_BYTES)


_TAIL_INPUTS = {None: 0, "final": 1, "kv": 3, "q": 3}
_TAIL_OUTPUTS = {None: 0, "final": 0, "kv": 2, "q": 1}


def _ffn_kernel(*refs, relayout, with_attn, tail, cast_next):
    refs = list(refs)
    x_ref = refs.pop(0)
    if with_attn:
        o_ref, wo_ref = refs.pop(0), refs.pop(0)
    g_ref, win_ref, wout_ref = refs.pop(0), refs.pop(0), refs.pop(0)
    tail_refs = [refs.pop(0) for _ in range(_TAIL_INPUTS[tail])]
    cast_srcs = [refs.pop(0) for _ in range(2 if cast_next else 0)]
    out_ref = refs.pop(0)
    tail_outs = [refs.pop(0) for _ in range(_TAIL_OUTPUTS[tail])]
    for src_ref, dst_ref in zip(cast_srcs, refs, strict=True):
        dst_ref[...] = src_ref[...].astype(BF16)

    for sub in range(FFN_SUBTILES):
        t0, r0 = sub * SUB_STEPS, sub * SUB_ROWS
        steps, rws = slice(t0, t0 + SUB_STEPS), slice(r0, r0 + SUB_ROWS)
        if relayout == "bt_to_tb":
            x = jnp.swapaxes(x_ref[:, steps, :], 0, 1).reshape(SUB_ROWS, D_MODEL)
        elif relayout == "tb_to_bt":
            x = jnp.swapaxes(x_ref[rws, :].reshape(SUB_STEPS, BATCH, D_MODEL), 0, 1)
            x = x.reshape(SUB_ROWS, D_MODEL)
        else:
            x = x_ref[rws, :]
        if with_attn:
            x = x + jnp.dot(o_ref[rws, :], wo_ref[...], preferred_element_type=F32)
        h = _rms(x, g_ref[...]).astype(BF16)
        acc = None
        for c0, c1 in FFN_CHUNKS:
            gate = jnp.dot(h, win_ref[:, c0:c1], preferred_element_type=F32)
            up = jnp.dot(h, win_ref[:, D_FF + c0:D_FF + c1], preferred_element_type=F32)
            a = (jax.nn.silu(gate) * up).astype(BF16)
            part = jnp.dot(a, wout_ref[c0:c1, :], preferred_element_type=F32)
            acc = part if acc is None else acc + part
        y = x + 0.5 * acc
        cube = (BATCH, SUB_STEPS, -1)
        if tail == "final":
            (gf_ref,) = tail_refs
            y = _rms(y, gf_ref[...])
        elif tail == "kv":
            gin_ref, wd_ref, glat_ref = tail_refs
            ckv_ref, kr_ref = tail_outs
            kv_a = jnp.dot(_rms(y, gin_ref[...]).astype(BF16), wd_ref[...], preferred_element_type=F32)
            ckv_ref[:, steps, :] = _rms(kv_a[:, :KV_LORA], glat_ref[...]).astype(BF16).reshape(cube)
            kr_ref[:, steps, :] = kv_a[:, KV_LORA:].reshape(cube)
        elif tail == "q":
            gm_ref, wdq_ref, gq_ref = tail_refs
            (cq_ref,) = tail_outs
            c_q = jnp.dot(_rms(y, gm_ref[...]).astype(BF16), wdq_ref[...], preferred_element_type=F32)
            cq_ref[rws, :] = _rms(c_q, gq_ref[...]).astype(BF16)
        if relayout == "tb_to_bt":
            out_ref[:, steps, :] = y.reshape(cube)
        else:
            out_ref[rws, :] = y


def _ffn(x, gain, w_in, w_out, relayout=None, attn=None, tail=None, tail_args=(), cast_next=None):
    rows = BATCH * SEQ
    n_steps = rows // FFN_ROWS
    row_spec = pl.BlockSpec((FFN_ROWS, D_MODEL), lambda i: (i, 0))
    cube_spec = pl.BlockSpec((BATCH, FFN_STEPS, D_MODEL), lambda i: (0, i, 0))
    flat = jax.ShapeDtypeStruct((rows, D_MODEL), F32)
    cube = jax.ShapeDtypeStruct((BATCH, SEQ, D_MODEL), F32)
    in_spec = cube_spec if relayout == "bt_to_tb" else row_spec
    out_spec, out_shape = (cube_spec, cube) if relayout == "tb_to_bt" else (row_spec, flat)

    args, specs = [x], [in_spec]
    if attn is not None:
        o, w_o = attn
        args += [o, w_o]
        specs += [row_spec, _resident(w_o.shape)]
    args += [gain, w_in, w_out]
    specs += [_resident(gain.shape), _resident(w_in.shape), _resident(w_out.shape)]
    assert len(tail_args) == _TAIL_INPUTS[tail]
    args += list(tail_args)
    specs += [_resident(a.shape) for a in tail_args]
    out_specs, out_shapes = [out_spec], [out_shape]
    if tail == "kv":
        assert relayout == "tb_to_bt"
        out_specs += [pl.BlockSpec((BATCH, FFN_STEPS, LANES), lambda i: (0, i, 0))] * 2
        out_shapes += [jax.ShapeDtypeStruct((BATCH, SEQ, KV_LORA), BF16),
                       jax.ShapeDtypeStruct((BATCH, SEQ, LANES), F32)]
    elif tail == "q":
        out_specs.append(pl.BlockSpec((FFN_ROWS, Q_LORA), lambda i: (i, 0)))
        out_shapes.append(jax.ShapeDtypeStruct((rows, Q_LORA), BF16))
    if cast_next is not None:
        src_in, src_out, layer, half = cast_next
        for src in (src_in, src_out):
            n_rows, n_cols = src.shape[2:]
            slab = n_rows // n_steps
            assert slab * n_steps == n_rows and slab % 16 == 0
            args.append(src)
            specs.append(pl.BlockSpec((None, None, slab, n_cols), lambda i: (layer, half, i, 0)))
            out_specs.append(pl.BlockSpec((slab, n_cols), lambda i: (i, 0)))
            out_shapes.append(jax.ShapeDtypeStruct((n_rows, n_cols), BF16))
    return pl.pallas_call(
        functools.partial(_ffn_kernel, relayout=relayout, with_attn=attn is not None, tail=tail,
                          cast_next=cast_next is not None),
        grid=(n_steps,),
        in_specs=specs,
        out_specs=out_specs,
        out_shape=out_shapes,
        compiler_params=_params(1),
        name="ffn",
    )(*args)


def _s5_kernel(x_ref, g_ref, wb_ref, lam_ref, wc_ref, d_ref, wglu_ref, out_ref,
               bu_ref, state_ref):
    @pl.when(pl.program_id(0) == 0)
    def _():
        state_ref[...] = jnp.zeros_like(state_ref)

    x = x_ref[...]
    h = _rms(x, g_ref[...])
    hb = h.astype(BF16)

    def project_in(kt):
        bu_ref[kt] = jnp.dot(hb[:, kt * MXU_DIM:(kt + 1) * MXU_DIM], wb_ref[kt],
                             preferred_element_type=F32)

    def scan(kt):
        cols = [(half * S5_HALF, S5_MODES + half * S5_HALF) for half in range(2)]
        state = [(state_ref[kt, :, re0:re0 + S5_HALF], state_ref[kt, :, im0:im0 + S5_HALF])
                 for re0, im0 in cols]
        for t in range(S5_STEPS):
            r0 = t * SUBLANES
            for half, (re0, im0) in enumerate(cols):
                lam_re = lam_ref[kt, :, re0:re0 + S5_HALF]
                lam_im = lam_ref[kt, :, im0:im0 + S5_HALF]
                s_re, s_im = state[half]
                n_re = lam_re * s_re - lam_im * s_im + bu_ref[kt, r0:r0 + SUBLANES, re0:re0 + S5_HALF]
                n_im = lam_re * s_im + lam_im * s_re + bu_ref[kt, r0:r0 + SUBLANES, im0:im0 + S5_HALF]
                bu_ref[kt, r0:r0 + SUBLANES, re0:re0 + S5_HALF] = n_re
                bu_ref[kt, r0:r0 + SUBLANES, im0:im0 + S5_HALF] = n_im
                state[half] = (n_re, n_im)
        for (re0, im0), (s_re, s_im) in zip(cols, state):
            state_ref[kt, :, re0:re0 + S5_HALF] = s_re
            state_ref[kt, :, im0:im0 + S5_HALF] = s_im

    ys = []
    project_in(0)
    for kt in range(S5_KT):
        if kt + 1 < S5_KT:
            project_in(kt + 1)
        scan(kt)
        ys.append(jnp.dot(bu_ref[kt].astype(BF16), wc_ref[kt], preferred_element_type=F32))
    y = jnp.concatenate(ys, axis=1) + d_ref[...] * h
    y = jax.nn.gelu(y, approximate=True).astype(BF16)
    z = jnp.dot(y, wglu_ref[...], preferred_element_type=F32)
    out_ref[...] = x + z[:, :D_MODEL] * jax.nn.sigmoid(z[:, D_MODEL:])


def _s5(x, gain, wb, lam, wc, d_skip, w_glu):
    rows = x.shape[0]
    row_spec = pl.BlockSpec((S5_ROWS, D_MODEL), lambda i: (i, 0))
    return pl.pallas_call(
        _s5_kernel,
        grid=(rows // S5_ROWS,),
        in_specs=[row_spec, _resident(gain.shape), _resident(wb.shape), _resident(lam.shape),
                  _resident(wc.shape), _resident(d_skip.shape), _resident(w_glu.shape)],
        out_specs=row_spec,
        out_shape=jax.ShapeDtypeStruct((rows, D_MODEL), F32),
        scratch_shapes=[pltpu.VMEM((S5_KT, S5_ROWS, 2 * S5_MODES), F32),
                        pltpu.VMEM((S5_KT, SUBLANES, 2 * S5_MODES), F32)],
        compiler_params=_params(1),
        name="s5_mixer",
    )(x, gain, wb, lam, wc, d_skip, w_glu)


def _s5_disc_kernel(lam_re_ref, lam_im_ref, lam_re_rep_ref, lam_im_rep_ref, log_step_ref,
                    b_re_ref, b_im_ref, lbar_re_ref, lbar_im_ref, bbar_re_ref, bbar_im_ref):
    step = jnp.exp(log_step_ref[...])

    def lam_bar(lam_re, lam_im):
        mag = jnp.exp(lam_re * step)
        return mag * jnp.cos(lam_im * step), mag * jnp.sin(lam_im * step)

    lbar_re_ref[...], lbar_im_ref[...] = lam_bar(lam_re_ref[...], lam_im_ref[...])
    lam_re, lam_im = lam_re_rep_ref[...], lam_im_rep_ref[...]
    lb_re, lb_im = lam_bar(lam_re, lam_im)
    n_re, n_im = lb_re - 1.0, lb_im
    den = lam_re * lam_re + lam_im * lam_im
    f_re = (n_re * lam_re + n_im * lam_im) / den
    f_im = (n_im * lam_re - n_re * lam_im) / den
    b_re, b_im = b_re_ref[...], b_im_ref[...]
    bbar_re_ref[...] = f_re * b_re - f_im * b_im
    bbar_im_ref[...] = f_re * b_im + f_im * b_re


def _s5_weights(lam_re, lam_im, log_step, b_re, b_im, c_re, c_im):
    pc = S5_STATE * S5_GROUP
    gp = jax.ShapeDtypeStruct((S5_GROUPS, S5_STATE), F32)
    gpc = jax.ShapeDtypeStruct((S5_GROUPS, pc), F32)
    lbar_re, lbar_im, bbar_re, bbar_im = pl.pallas_call(
        _s5_disc_kernel, out_shape=[gp, gp, gpc, gpc], name="s5_discretise",
    )(lam_re.astype(F32), lam_im.astype(F32),
      jnp.repeat(lam_re.astype(F32), S5_GROUP, axis=1), jnp.repeat(lam_im.astype(F32), S5_GROUP, axis=1),
      log_step.astype(F32).reshape(S5_GROUPS, 1),
      b_re.astype(F32).reshape(S5_GROUPS, pc), b_im.astype(F32).reshape(S5_GROUPS, pc))
    eye = jnp.eye(S5_GPT, dtype=F32)

    def pack_b(part):
        part = part.reshape(S5_KT, S5_GPT, S5_STATE, S5_GROUP)
        return jnp.einsum('kgpc,gh->kgchp', part, eye).reshape(S5_KT, MXU_DIM, S5_MODES)

    def pack_c(part):
        part = part.reshape(S5_KT, S5_GPT, S5_GROUP, S5_STATE)
        return jnp.einsum('kgcp,gh->kgphc', part, eye).reshape(S5_KT, S5_MODES, MXU_DIM)

    wb = jnp.concatenate([pack_b(bbar_re), pack_b(bbar_im)], axis=-1)
    wc = jnp.concatenate([pack_c(c_re.astype(F32)), -pack_c(c_im.astype(F32))], axis=1)
    lam_row = jnp.concatenate([lbar_re.reshape(S5_KT, S5_MODES),
                               lbar_im.reshape(S5_KT, S5_MODES)], axis=-1)
    lam_tile = jnp.broadcast_to(lam_row[:, None, :], (S5_KT, SUBLANES, 2 * S5_MODES))
    return wb.astype(BF16), lam_tile, wc.astype(BF16)


def _attn_kernel(cq_ref, ckv_ref, kr_ref, rot_ref, wq_ref, wkv_ref, o_ref):
    scale = math.log2(math.e) / math.sqrt(NOPE_DIM + ROPE_DIM)
    rot = rot_ref[...]
    low_lanes = lax.broadcasted_iota(jnp.int32, (SEQ, LANES), 1) < ROPE_DIM

    def rope(parts):
        t = parts * rot
        return jnp.where(low_lanes, t + pltpu.roll(t, ROPE_DIM, axis=1), 0.0)

    k_rope = rope(kr_ref[...]).astype(BF16)
    ones = jnp.ones((SEQ, V_DIM), BF16)
    row_chunk = lax.broadcasted_iota(jnp.int32, (ATT_Q, ATT_Q), 0) // CHUNK
    col_chunk = lax.broadcasted_iota(jnp.int32, (ATT_Q, ATT_Q), 1) // CHUNK
    diag_mask = col_chunk <= row_chunk
    neg = jnp.finfo(F32).min

    for hd in range(ATT_HEADS):
        q = jnp.dot(cq_ref[...], wq_ref[hd], preferred_element_type=F32)
        q = jnp.concatenate([(q[:, :NOPE_DIM] * scale).astype(BF16),
                             (rope(q[:, NOPE_DIM:]) * scale).astype(BF16)], axis=1)
        kv = jnp.dot(ckv_ref[...], wkv_ref[hd], preferred_element_type=F32)
        k = jnp.concatenate([kv[:, :NOPE_DIM].astype(BF16), k_rope], axis=1)
        v_ones = jnp.concatenate([kv[:, NOPE_DIM:].astype(BF16), ones], axis=1)
        for i in reversed(range(SEQ // ATT_Q)):
            q0, n_keys = i * ATT_Q, (i + 1) * ATT_Q
            s = lax.dot_general(q[q0:q0 + ATT_Q, :], k[0:n_keys, :],
                                (((1,), (1,)), ((), ())), preferred_element_type=F32)
            s_diag = jnp.where(diag_mask, s[:, q0:n_keys], neg)
            s = s_diag if i == 0 else jnp.concatenate([s[:, :q0], s_diag], axis=1)
            m = jnp.max(s, axis=-1, keepdims=True)
            p = jnp.exp2(s - m).astype(BF16)
            o = jnp.dot(p, v_ones[0:n_keys, :], preferred_element_type=F32)
            o_ref[q0:q0 + ATT_Q, hd * V_DIM:(hd + 1) * V_DIM] = (
                o[:, :V_DIM] / o[:, V_DIM:]).astype(BF16)


def _attention(c_q, c_kv, k_parts, rot_table, wq, wkv):
    def per_batch(width):
        return pl.BlockSpec((None, SEQ, width), lambda b, h: (b, 0, 0))

    def per_head_group(w):
        return pl.BlockSpec((ATT_HEADS,) + w.shape[1:], lambda b, h: (h, 0, 0))

    return pl.pallas_call(
        _attn_kernel,
        grid=(BATCH, MLA_HEADS // ATT_HEADS),
        in_specs=[per_batch(Q_LORA), per_batch(KV_LORA), per_batch(LANES), per_batch(LANES),
                  per_head_group(wq), per_head_group(wkv)],
        out_specs=pl.BlockSpec((None, SEQ, ATT_HEADS * V_DIM), lambda b, h: (b, 0, h)),
        out_shape=jax.ShapeDtypeStruct((BATCH, SEQ, MLA_HEADS * V_DIM), BF16),
        compiler_params=_params(2),
        name="mla_attention",
    )(c_q, c_kv, k_parts, rot_table, wq, wkv)


def _rope_table(positions):
    inv_freq = 1.0 / (ROPE_BASE ** (jnp.arange(0, ROPE_DIM, 2, dtype=F32) / ROPE_DIM))
    ang = positions.astype(F32)[..., None] * inv_freq
    cos, sin = jnp.cos(ang), jnp.sin(ang)
    return jnp.concatenate([cos, cos, -sin, sin], axis=-1)


def _with_rot_cols(w):
    half = ROPE_DIM // 2
    return jnp.concatenate([w, w[..., half:], w[..., :half]], axis=-1)


def kernel(x, positions, norm_gains, ffn_w_in, ffn_w_out, s5_lambda_re, s5_lambda_im, s5_log_step, s5_b_re, s5_b_im, s5_c_re, s5_c_im, s5_d, s5_w_glu, kv_in_norm, w_dkv, kv_latent_norm, w_ukv, mla_w_dq, mla_q_norm, mla_w_uq, mla_w_o, final_norm):
    assert x.shape == (BATCH, SEQ, D_MODEL) and norm_gains.shape[0] == 2
    rows = BATCH * SEQ
    gains = norm_gains.astype(F32).reshape(2, 3, 1, D_MODEL)
    w_in_f32, w_out_f32 = ffn_w_in.astype(F32), ffn_w_out.astype(F32)

    xt, w_in, w_out = _ffn(x.astype(F32), gains[0, 0], w_in_f32[0, 0].astype(BF16),
                           w_out_f32[0, 0].astype(BF16), relayout="bt_to_tb",
                           cast_next=(w_in_f32, w_out_f32, 0, 1))
    wb, lam_tile, wc = _s5_weights(s5_lambda_re[0], s5_lambda_im[0], s5_log_step[0],
                                   s5_b_re[0], s5_b_im[0], s5_c_re[0], s5_c_im[0])
    xt = _s5(xt, gains[0, 1], wb, lam_tile, wc, s5_d[0].astype(F32).reshape(1, D_MODEL),
             s5_w_glu[0].astype(BF16))
    wd = jnp.concatenate([w_dkv[:, :KV_LORA], _with_rot_cols(w_dkv[:, KV_LORA:])], axis=-1)
    xb, c_kv, k_parts, w_in, w_out = _ffn(
        xt, gains[0, 2], w_in, w_out, relayout="tb_to_bt", tail="kv",
        tail_args=(kv_in_norm.astype(F32).reshape(1, D_MODEL), wd.astype(BF16),
                   kv_latent_norm.astype(F32).reshape(1, KV_LORA)),
        cast_next=(w_in_f32, w_out_f32, 1, 0))
    xb = xb.reshape(rows, D_MODEL)

    xb, c_q, w_in, w_out = _ffn(xb, gains[1, 0], w_in, w_out, tail="q",
                                tail_args=(gains[1, 1], mla_w_dq[0].astype(BF16),
                                           mla_q_norm[0].astype(F32).reshape(1, Q_LORA)),
                                cast_next=(w_in_f32, w_out_f32, 1, 1))
    w_uq_h = mla_w_uq[0].reshape(Q_LORA, MLA_HEADS, NOPE_DIM + ROPE_DIM)
    wq = jnp.concatenate([w_uq_h[..., :NOPE_DIM], _with_rot_cols(w_uq_h[..., NOPE_DIM:])], axis=-1)
    wq = wq.transpose(1, 0, 2).astype(BF16)
    wkv = w_ukv.reshape(KV_LORA, MLA_HEADS, NOPE_DIM + V_DIM).transpose(1, 0, 2).astype(BF16)
    o = _attention(c_q.reshape(BATCH, SEQ, Q_LORA), c_kv, k_parts, _rope_table(positions), wq, wkv)
    (out,) = _ffn(xb, gains[1, 2], w_in, w_out,
                  attn=(o.reshape(rows, MLA_HEADS * V_DIM), mla_w_o[0].astype(BF16)),
                  tail="final", tail_args=(final_norm.astype(F32).reshape(1, D_MODEL),))
    return out.reshape(BATCH, SEQ, D_MODEL).astype(x.dtype)
```

```python
import functools
import math

import jax
import jax.numpy as jnp
from jax import lax
from jax.experimental import pallas as pl
from jax.experimental.pallas import tpu as pltpu

D_MODEL = 1024
BATCH = 8
SEQ = 2048
CHUNK = 64
EPS = 1e-6
D_FF = 2816
S5_GROUP = 16
S5_GROUPS = D_MODEL // S5_GROUP
S5_STATE = 64
MLA_HEADS = 8
NOPE_DIM = 128
ROPE_DIM = 64
V_DIM = 128
Q_LORA = 256
KV_LORA = 128
ROPE_BASE = 10000.0

LANES = 128
SUBLANES = 8
MXU_DIM = 256
VMEM_LIMIT_BYTES = 56 * 1024 * 1024

HEAD_PAD = 2 * LANES
FFN_STEPS = 128
FFN_ROWS = FFN_STEPS * BATCH
FFN_SUBTILES = 2
SUB_STEPS = FFN_STEPS // FFN_SUBTILES
SUB_ROWS = FFN_ROWS // FFN_SUBTILES
FFN_CHUNKS = ((0, 1024), (1024, 2048), (2048, 2816))
S5_STEPS = 64
S5_ROWS = S5_STEPS * BATCH
S5_KT = D_MODEL // MXU_DIM
S5_GPT = MXU_DIM // S5_GROUP
S5_MODES = S5_GPT * S5_STATE
S5_HALF = S5_MODES // 2
ATT_Q = 256
ATT_HEADS = 1

BF16 = jnp.bfloat16
F32 = jnp.float32


def _rms(x, gain):
    return x * lax.rsqrt(jnp.mean(x * x, axis=-1, keepdims=True) + EPS) * gain


def _resident(shape):
    zeros = (0,) * len(shape)
    return pl.BlockSpec(shape, lambda *_: zeros, pipeline_mode=pl.Buffered(1))


def _params(n_grid_dims):
    return pltpu.CompilerParams(
        dimension_semantics=("arbitrary",) * n_grid_dims,
        vmem_limit_bytes=VMEM_LIMIT_BYTES)


_TAIL_INPUTS = {None: 0, "final": 1, "kv": 3, "q": 3}
_TAIL_OUTPUTS = {None: 0, "final": 0, "kv": 2, "q": 1}


def _ffn_kernel(*refs, relayout, with_attn, tail, cast_next):
    refs = list(refs)
    x_ref = refs.pop(0)
    if with_attn:
        o_ref, wo_ref = refs.pop(0), refs.pop(0)
    g_ref, win_ref, wout_ref = refs.pop(0), refs.pop(0), refs.pop(0)
    tail_refs = [refs.pop(0) for _ in range(_TAIL_INPUTS[tail])]
    cast_srcs = [refs.pop(0) for _ in range(2 if cast_next else 0)]
    out_ref = refs.pop(0)
    tail_outs = [refs.pop(0) for _ in range(_TAIL_OUTPUTS[tail])]
    for src_ref, dst_ref in zip(cast_srcs, refs, strict=True):
        dst_ref[...] = src_ref[...].astype(BF16)

    for sub in range(FFN_SUBTILES):
        t0, r0 = sub * SUB_STEPS, sub * SUB_ROWS
        steps, rws = slice(t0, t0 + SUB_STEPS), slice(r0, r0 + SUB_ROWS)
        if relayout == "bt_to_tb":
            x = jnp.swapaxes(x_ref[:, steps, :], 0, 1).reshape(SUB_ROWS, D_MODEL)
        elif relayout == "tb_to_bt":
            x = jnp.swapaxes(x_ref[rws, :].reshape(SUB_STEPS, BATCH, D_MODEL), 0, 1)
            x = x.reshape(SUB_ROWS, D_MODEL)
        else:
            x = x_ref[rws, :]
        if with_attn:
            x = x + jnp.dot(o_ref[rws, :], wo_ref[...], preferred_element_type=F32)
        h = _rms(x, g_ref[...]).astype(BF16)
        acc = None
        for c0, c1 in FFN_CHUNKS:
            gate = jnp.dot(h, win_ref[:, c0:c1], preferred_element_type=F32)
            up = jnp.dot(h, win_ref[:, D_FF + c0:D_FF + c1], preferred_element_type=F32)
            a = (jax.nn.silu(gate) * up).astype(BF16)
            part = jnp.dot(a, wout_ref[c0:c1, :], preferred_element_type=F32)
            acc = part if acc is None else acc + part
        y = x + 0.5 * acc
        cube = (BATCH, SUB_STEPS, -1)
        if tail == "final":
            (gf_ref,) = tail_refs
            y = _rms(y, gf_ref[...])
        elif tail == "kv":
            gin_ref, wd_ref, glat_ref = tail_refs
            ckv_ref, kr_ref = tail_outs
            kv_a = jnp.dot(_rms(y, gin_ref[...]).astype(BF16), wd_ref[...], preferred_element_type=F32)
            ckv_ref[:, steps, :] = _rms(kv_a[:, :KV_LORA], glat_ref[...]).astype(BF16).reshape(cube)
            kr_ref[:, steps, :] = kv_a[:, KV_LORA:].reshape(cube)
        elif tail == "q":
            gm_ref, wdq_ref, gq_ref = tail_refs
            (cq_ref,) = tail_outs
            c_q = jnp.dot(_rms(y, gm_ref[...]).astype(BF16), wdq_ref[...], preferred_element_type=F32)
            cq_ref[rws, :] = _rms(c_q, gq_ref[...]).astype(BF16)
        if relayout == "tb_to_bt":
            out_ref[:, steps, :] = y.reshape(cube)
        else:
            out_ref[rws, :] = y


def _ffn(x, gain, w_in, w_out, relayout=None, attn=None, tail=None, tail_args=(), cast_next=None):
    rows = BATCH * SEQ
    n_steps = rows // FFN_ROWS
    row_spec = pl.BlockSpec((FFN_ROWS, D_MODEL), lambda i: (i, 0))
    cube_spec = pl.BlockSpec((BATCH, FFN_STEPS, D_MODEL), lambda i: (0, i, 0))
    flat = jax.ShapeDtypeStruct((rows, D_MODEL), F32)
    cube = jax.ShapeDtypeStruct((BATCH, SEQ, D_MODEL), F32)
    in_spec = cube_spec if relayout == "bt_to_tb" else row_spec
    out_spec, out_shape = (cube_spec, cube) if relayout == "tb_to_bt" else (row_spec, flat)

    args, specs = [x], [in_spec]
    if attn is not None:
        o, w_o = attn
        args += [o, w_o]
        specs += [row_spec, _resident(w_o.shape)]
    args += [gain, w_in, w_out]
    specs += [_resident(gain.shape), _resident(w_in.shape), _resident(w_out.shape)]
    assert len(tail_args) == _TAIL_INPUTS[tail]
    args += list(tail_args)
    specs += [_resident(a.shape) for a in tail_args]
    out_specs, out_shapes = [out_spec], [out_shape]
    if tail == "kv":
        assert relayout == "tb_to_bt"
        out_specs += [pl.BlockSpec((BATCH, FFN_STEPS, LANES), lambda i: (0, i, 0))] * 2
        out_shapes += [jax.ShapeDtypeStruct((BATCH, SEQ, KV_LORA), BF16),
                       jax.ShapeDtypeStruct((BATCH, SEQ, LANES), F32)]
    elif tail == "q":
        out_specs.append(pl.BlockSpec((FFN_ROWS, Q_LORA), lambda i: (i, 0)))
        out_shapes.append(jax.ShapeDtypeStruct((rows, Q_LORA), BF16))
    if cast_next is not None:
        src_in, src_out, layer, half = cast_next
        for src in (src_in, src_out):
            n_rows, n_cols = src.shape[2:]
            slab = n_rows // n_steps
            assert slab * n_steps == n_rows and slab % 16 == 0
            args.append(src)
            specs.append(pl.BlockSpec((None, None, slab, n_cols), lambda i: (layer, half, i, 0)))
            out_specs.append(pl.BlockSpec((slab, n_cols), lambda i: (i, 0)))
            out_shapes.append(jax.ShapeDtypeStruct((n_rows, n_cols), BF16))
    return pl.pallas_call(
        functools.partial(_ffn_kernel, relayout=relayout, with_attn=attn is not None, tail=tail,
                          cast_next=cast_next is not None),
        grid=(n_steps,),
        in_specs=specs,
        out_specs=out_specs,
        out_shape=out_shapes,
        compiler_params=_params(1),
        name="ffn",
    )(*args)


def _s5_kernel(x_ref, g_ref, wb_ref, lam_ref, wc_ref, d_ref, wglu_ref, out_ref,
               bu_ref, state_ref):
    @pl.when(pl.program_id(0) == 0)
    def _():
        state_ref[...] = jnp.zeros_like(state_ref)

    x = x_ref[...]
    h = _rms(x, g_ref[...])
    hb = h.astype(BF16)

    def project_in(kt):
        bu_ref[kt] = jnp.dot(hb[:, kt * MXU_DIM:(kt + 1) * MXU_DIM], wb_ref[kt],
                             preferred_element_type=F32)

    def scan(kt):
        cols = [(half * S5_HALF, S5_MODES + half * S5_HALF) for half in range(2)]
        state = [(state_ref[kt, :, re0:re0 + S5_HALF], state_ref[kt, :, im0:im0 + S5_HALF])
                 for re0, im0 in cols]
        for t in range(S5_STEPS):
            r0 = t * SUBLANES
            for half, (re0, im0) in enumerate(cols):
                lam_re = lam_ref[kt, :, re0:re0 + S5_HALF]
                lam_im = lam_ref[kt, :, im0:im0 + S5_HALF]
                s_re, s_im = state[half]
                n_re = lam_re * s_re - lam_im * s_im + bu_ref[kt, r0:r0 + SUBLANES, re0:re0 + S5_HALF]
                n_im = lam_re * s_im + lam_im * s_re + bu_ref[kt, r0:r0 + SUBLANES, im0:im0 + S5_HALF]
                bu_ref[kt, r0:r0 + SUBLANES, re0:re0 + S5_HALF] = n_re
                bu_ref[kt, r0:r0 + SUBLANES, im0:im0 + S5_HALF] = n_im
                state[half] = (n_re, n_im)
        for (re0, im0), (s_re, s_im) in zip(cols, state):
            state_ref[kt, :, re0:re0 + S5_HALF] = s_re
            state_ref[kt, :, im0:im0 + S5_HALF] = s_im

    ys = []
    project_in(0)
    for kt in range(S5_KT):
        if kt + 1 < S5_KT:
            project_in(kt + 1)
        scan(kt)
        ys.append(jnp.dot(bu_ref[kt].astype(BF16), wc_ref[kt], preferred_element_type=F32))
    y = jnp.concatenate(ys, axis=1) + d_ref[...] * h
    y = jax.nn.gelu(y, approximate=True).astype(BF16)
    z = jnp.dot(y, wglu_ref[...], preferred_element_type=F32)
    out_ref[...] = x + z[:, :D_MODEL] * jax.nn.sigmoid(z[:, D_MODEL:])


def _s5(x, gain, wb, lam, wc, d_skip, w_glu):
    rows = x.shape[0]
    row_spec = pl.BlockSpec((S5_ROWS, D_MODEL), lambda i: (i, 0))
    return pl.pallas_call(
        _s5_kernel,
        grid=(rows // S5_ROWS,),
        in_specs=[row_spec, _resident(gain.shape), _resident(wb.shape), _resident(lam.shape),
                  _resident(wc.shape), _resident(d_skip.shape), _resident(w_glu.shape)],
        out_specs=row_spec,
        out_shape=jax.ShapeDtypeStruct((rows, D_MODEL), F32),
        scratch_shapes=[pltpu.VMEM((S5_KT, S5_ROWS, 2 * S5_MODES), F32),
                        pltpu.VMEM((S5_KT, SUBLANES, 2 * S5_MODES), F32)],
        compiler_params=_params(1),
        name="s5_mixer",
    )(x, gain, wb, lam, wc, d_skip, w_glu)


def _s5_pack_kernel(lam_gc_ref, step_gc_ref, b_ref, c_ref, lam_row_ref, step_row_ref,
                    wb_ref, wc_ref, lam_tile_ref):
    def lam_bar(lam_re, lam_im, log_step):
        step = jnp.exp(log_step)
        mag = jnp.exp(lam_re * step)
        return mag * jnp.cos(lam_im * step), mag * jnp.sin(lam_im * step)

    def spread(x, width):
        k = x.shape[1]
        rep = (lax.broadcasted_iota(jnp.int32, (k, width), 1) % k
               == lax.broadcasted_iota(jnp.int32, (k, width), 0)).astype(F32)
        return jnp.dot(x, rep, preferred_element_type=F32)

    lbar_re, lbar_im = lam_bar(lam_row_ref[0], lam_row_ref[1], step_row_ref[...])
    tile = (S5_KT, SUBLANES, S5_MODES)
    lam_tile_ref[:, :, :S5_MODES] = jnp.broadcast_to(lbar_re[:, None, :], tile)
    lam_tile_ref[:, :, S5_MODES:] = jnp.broadcast_to(lbar_im[:, None, :], tile)

    lam_re, lam_im = lam_gc_ref[0], lam_gc_ref[1]
    lb_re, lb_im = lam_bar(lam_re, lam_im, step_gc_ref[...])
    n_re, n_im = lb_re - 1.0, lb_im
    den = lam_re * lam_re + lam_im * lam_im
    f_re = (n_re * lam_re + n_im * lam_im) / den
    f_im = (n_im * lam_re - n_re * lam_im) / den
    b_re, b_im = b_ref[0], b_ref[1]
    shape = (S5_GROUPS * S5_GROUP, S5_MODES)
    own = (lax.broadcasted_iota(jnp.int32, shape, 0) // S5_GROUP % S5_GPT
           == lax.broadcasted_iota(jnp.int32, shape, 1) // S5_STATE)
    for part, bbar in enumerate((f_re * b_re - f_im * b_im, f_re * b_im + f_im * b_re)):
        wb_ref[:, part * S5_MODES:(part + 1) * S5_MODES] = jnp.where(
            own, spread(bbar, S5_MODES), 0.0).astype(BF16)

    shape = (S5_GROUPS * S5_STATE, MXU_DIM)
    own = (lax.broadcasted_iota(jnp.int32, shape, 0) // S5_STATE % S5_GPT
           == lax.broadcasted_iota(jnp.int32, shape, 1) // S5_GROUP)
    for part, sign in enumerate((1.0, -1.0)):
        packed = jnp.where(own, sign * spread(c_ref[part], MXU_DIM), 0.0).astype(BF16)
        for kt in range(S5_KT):
            wc_ref[kt, part] = packed[kt * S5_MODES:(kt + 1) * S5_MODES]


def _s5_weights(lam_re, lam_im, log_step, b_re, b_im, c_re, c_im):
    lam = jnp.stack([lam_re, lam_im]).astype(F32)
    step = log_step.astype(F32)
    lam_gc = jnp.repeat(lam, S5_GROUP, axis=1)
    step_gc = jnp.repeat(step, S5_GROUP).reshape(S5_GROUPS * S5_GROUP, 1)
    b_gc = jnp.stack([b_re, b_im]).astype(F32).transpose(0, 1, 3, 2)
    b_gc = b_gc.reshape(2, S5_GROUPS * S5_GROUP, S5_STATE)
    c_gp = jnp.stack([c_re, c_im]).astype(F32).transpose(0, 1, 3, 2)
    c_gp = c_gp.reshape(2, S5_GROUPS * S5_STATE, S5_GROUP)
    lam_row = lam.reshape(2, S5_KT, S5_MODES)
    step_row = jnp.repeat(step, S5_STATE).reshape(S5_KT, S5_MODES)
    wb, wc, lam_tile = pl.pallas_call(
        _s5_pack_kernel,
        out_shape=[jax.ShapeDtypeStruct((S5_GROUPS * S5_GROUP, 2 * S5_MODES), BF16),
                   jax.ShapeDtypeStruct((S5_KT, 2, S5_MODES, MXU_DIM), BF16),
                   jax.ShapeDtypeStruct((S5_KT, SUBLANES, 2 * S5_MODES), F32)],
        compiler_params=pltpu.CompilerParams(vmem_limit_bytes=VMEM_LIMIT_BYTES),
        name="s5_pack",
    )(lam_gc, step_gc, b_gc, c_gp, lam_row, step_row)
    return (wb.reshape(S5_KT, MXU_DIM, 2 * S5_MODES), lam_tile,
            wc.reshape(S5_KT, 2 * S5_MODES, MXU_DIM))


def _attn_kernel(cq_ref, ckv_ref, kr_ref, rot_ref, wq_ref, wkv_ref, o_ref):
    scale = math.log2(math.e) / math.sqrt(NOPE_DIM + ROPE_DIM)
    rot = rot_ref[...]
    low_lanes = lax.broadcasted_iota(jnp.int32, (SEQ, LANES), 1) < ROPE_DIM

    def rope(parts):
        t = parts * rot
        return jnp.where(low_lanes, t + pltpu.roll(t, ROPE_DIM, axis=1), 0.0)

    k_rope = rope(kr_ref[...]).astype(BF16)
    ones = jnp.ones((SEQ, V_DIM), BF16)
    row_chunk = lax.broadcasted_iota(jnp.int32, (ATT_Q, ATT_Q), 0) // CHUNK
    col_chunk = lax.broadcasted_iota(jnp.int32, (ATT_Q, ATT_Q), 1) // CHUNK
    diag_mask = col_chunk <= row_chunk
    neg = jnp.finfo(F32).min

    for hd in range(ATT_HEADS):
        q = jnp.dot(cq_ref[...], wq_ref[hd], preferred_element_type=F32)
        q = jnp.concatenate([(q[:, :NOPE_DIM] * scale).astype(BF16),
                             (rope(q[:, NOPE_DIM:]) * scale).astype(BF16)], axis=1)
        kv = jnp.dot(ckv_ref[...], wkv_ref[hd], preferred_element_type=F32)
        k = jnp.concatenate([kv[:, :NOPE_DIM].astype(BF16), k_rope], axis=1)
        v_ones = jnp.concatenate([kv[:, NOPE_DIM:].astype(BF16), ones], axis=1)
        for i in reversed(range(SEQ // ATT_Q)):
            q0, n_keys = i * ATT_Q, (i + 1) * ATT_Q
            s = lax.dot_general(q[q0:q0 + ATT_Q, :], k[0:n_keys, :],
                                (((1,), (1,)), ((), ())), preferred_element_type=F32)
            s_diag = jnp.where(diag_mask, s[:, q0:n_keys], neg)
            s = s_diag if i == 0 else jnp.concatenate([s[:, :q0], s_diag], axis=1)
            m = jnp.max(s, axis=-1, keepdims=True)
            p = jnp.exp2(s - m).astype(BF16)
            o = jnp.dot(p, v_ones[0:n_keys, :], preferred_element_type=F32)
            o_ref[q0:q0 + ATT_Q, hd * V_DIM:(hd + 1) * V_DIM] = (
                o[:, :V_DIM] / o[:, V_DIM:]).astype(BF16)


def _attention(c_q, c_kv, k_parts, rot_table, wq, wkv):
    def per_batch(width):
        return pl.BlockSpec((None, SEQ, width), lambda b, h: (b, 0, 0))

    def per_head_group(w):
        return pl.BlockSpec((ATT_HEADS,) + w.shape[1:], lambda b, h: (h, 0, 0))

    return pl.pallas_call(
        _attn_kernel,
        grid=(BATCH, MLA_HEADS // ATT_HEADS),
        in_specs=[per_batch(Q_LORA), per_batch(KV_LORA), per_batch(LANES), per_batch(LANES),
                  per_head_group(wq), per_head_group(wkv)],
        out_specs=pl.BlockSpec((None, SEQ, ATT_HEADS * V_DIM), lambda b, h: (b, 0, h)),
        out_shape=jax.ShapeDtypeStruct((BATCH, SEQ, MLA_HEADS * V_DIM), BF16),
        compiler_params=_params(2),
        name="mla_attention",
    )(c_q, c_kv, k_parts, rot_table, wq, wkv)


def _rope_table(positions):
    inv_freq = 1.0 / (ROPE_BASE ** (jnp.arange(0, ROPE_DIM, 2, dtype=F32) / ROPE_DIM))
    ang = positions.astype(F32)[..., None] * inv_freq
    cos, sin = jnp.cos(ang), jnp.sin(ang)
    return jnp.concatenate([cos, cos, -sin, sin], axis=-1)


def _with_rot_cols(w):
    half = ROPE_DIM // 2
    return jnp.concatenate([w, w[..., half:], w[..., :half]], axis=-1)


def kernel(x, positions, norm_gains, ffn_w_in, ffn_w_out, s5_lambda_re, s5_lambda_im, s5_log_step, s5_b_re, s5_b_im, s5_c_re, s5_c_im, s5_d, s5_w_glu, kv_in_norm, w_dkv, kv_latent_norm, w_ukv, mla_w_dq, mla_q_norm, mla_w_uq, mla_w_o, final_norm):
    assert x.shape == (BATCH, SEQ, D_MODEL) and norm_gains.shape[0] == 2
    rows = BATCH * SEQ
    gains = norm_gains.astype(F32).reshape(2, 3, 1, D_MODEL)
    w_in_f32, w_out_f32 = ffn_w_in.astype(F32), ffn_w_out.astype(F32)

    xt, w_in, w_out = _ffn(x.astype(F32), gains[0, 0], w_in_f32[0, 0].astype(BF16),
                           w_out_f32[0, 0].astype(BF16), relayout="bt_to_tb",
                           cast_next=(w_in_f32, w_out_f32, 0, 1))
    wb, lam_tile, wc = _s5_weights(s5_lambda_re[0], s5_lambda_im[0], s5_log_step[0],
                                   s5_b_re[0], s5_b_im[0], s5_c_re[0], s5_c_im[0])
    xt = _s5(xt, gains[0, 1], wb, lam_tile, wc, s5_d[0].astype(F32).reshape(1, D_MODEL),
             s5_w_glu[0].astype(BF16))
    wd = jnp.concatenate([w_dkv[:, :KV_LORA], _with_rot_cols(w_dkv[:, KV_LORA:])], axis=-1)
    xb, c_kv, k_parts, w_in, w_out = _ffn(
        xt, gains[0, 2], w_in, w_out, relayout="tb_to_bt", tail="kv",
        tail_args=(kv_in_norm.astype(F32).reshape(1, D_MODEL), wd.astype(BF16),
                   kv_latent_norm.astype(F32).reshape(1, KV_LORA)),
        cast_next=(w_in_f32, w_out_f32, 1, 0))
    xb = xb.reshape(rows, D_MODEL)

    xb, c_q, w_in, w_out = _ffn(xb, gains[1, 0], w_in, w_out, tail="q",
                                tail_args=(gains[1, 1], mla_w_dq[0].astype(BF16),
                                           mla_q_norm[0].astype(F32).reshape(1, Q_LORA)),
                                cast_next=(w_in_f32, w_out_f32, 1, 1))
    w_uq_h = mla_w_uq[0].reshape(Q_LORA, MLA_HEADS, NOPE_DIM + ROPE_DIM)
    wq = jnp.concatenate([w_uq_h[..., :NOPE_DIM], _with_rot_cols(w_uq_h[..., NOPE_DIM:])], axis=-1)
    wq = wq.transpose(1, 0, 2).astype(BF16)
    wkv = w_ukv.reshape(KV_LORA, MLA_HEADS, NOPE_DIM + V_DIM).transpose(1, 0, 2).astype(BF16)
    o = _attention(c_q.reshape(BATCH, SEQ, Q_LORA), c_kv, k_parts, _rope_table(positions), wq, wkv)
    (out,) = _ffn(xb, gains[1, 2], w_in, w_out,
                  attn=(o.reshape(rows, MLA_HEADS * V_DIM), mla_w_o[0].astype(BF16)),
                  tail="final", tail_args=(final_norm.astype(F32).reshape(1, D_MODEL),))
    return out.reshape(BATCH, SEQ, D_MODEL).astype(x.dtype)
```

```python
import functools
import math

import jax
import jax.numpy as jnp
from jax import lax
from jax.experimental import pallas as pl
from jax.experimental.pallas import tpu as pltpu

D_MODEL = 1024
BATCH = 8
SEQ = 2048
CHUNK = 64
EPS = 1e-6
D_FF = 2816
S5_GROUP = 16
S5_GROUPS = D_MODEL // S5_GROUP
S5_STATE = 64
MLA_HEADS = 8
NOPE_DIM = 128
ROPE_DIM = 64
V_DIM = 128
Q_LORA = 256
KV_LORA = 128
ROPE_BASE = 10000.0

LANES = 128
SUBLANES = 8
MXU_DIM = 256
VMEM_LIMIT_BYTES = 56 * 1024 * 1024

HEAD_PAD = 2 * LANES
FFN_STEPS = 128
FFN_ROWS = FFN_STEPS * BATCH
FFN_SUBTILES = 2
SUB_STEPS = FFN_STEPS // FFN_SUBTILES
SUB_ROWS = FFN_ROWS // FFN_SUBTILES
FFN_CHUNKS = ((0, 1024), (1024, 2048), (2048, 2816))
S5_STEPS = 64
S5_ROWS = S5_STEPS * BATCH
S5_KT = D_MODEL // MXU_DIM
S5_GPT = MXU_DIM // S5_GROUP
S5_MODES = S5_GPT * S5_STATE
S5_HALF = S5_MODES // 2
ATT_Q = 256
ATT_HEADS = 2

BF16 = jnp.bfloat16
F32 = jnp.float32


def _rms(x, gain):
    return x * lax.rsqrt(jnp.mean(x * x, axis=-1, keepdims=True) + EPS) * gain


def _resident(shape):
    zeros = (0,) * len(shape)
    return pl.BlockSpec(shape, lambda *_: zeros, pipeline_mode=pl.Buffered(1))


def _params(n_grid_dims):
    return pltpu.CompilerParams(
        dimension_semantics=("arbitrary",) * n_grid_dims,
        vmem_limit_bytes=VMEM_LIMIT_BYTES)


_TAIL_INPUTS = {None: 0, "final": 1, "kv": 3, "q": 3}
_TAIL_OUTPUTS = {None: 0, "final": 0, "kv": 2, "q": 1}


def _ffn_kernel(*refs, relayout, with_attn, tail, cast_next):
    refs = list(refs)
    x_ref = refs.pop(0)
    if with_attn:
        o_ref, wo_ref = refs.pop(0), refs.pop(0)
    g_ref, win_ref, wout_ref = refs.pop(0), refs.pop(0), refs.pop(0)
    tail_refs = [refs.pop(0) for _ in range(_TAIL_INPUTS[tail])]
    cast_srcs = [refs.pop(0) for _ in range(2 if cast_next else 0)]
    out_ref = refs.pop(0)
    tail_outs = [refs.pop(0) for _ in range(_TAIL_OUTPUTS[tail])]
    for src_ref, dst_ref in zip(cast_srcs, refs, strict=True):
        dst_ref[...] = src_ref[...].astype(BF16)

    for sub in range(FFN_SUBTILES):
        t0, r0 = sub * SUB_STEPS, sub * SUB_ROWS
        steps, rws = slice(t0, t0 + SUB_STEPS), slice(r0, r0 + SUB_ROWS)
        if relayout == "bt_to_tb":
            x = jnp.swapaxes(x_ref[:, steps, :], 0, 1).reshape(SUB_ROWS, D_MODEL)
        elif relayout == "tb_to_bt":
            x = jnp.swapaxes(x_ref[rws, :].reshape(SUB_STEPS, BATCH, D_MODEL), 0, 1)
            x = x.reshape(SUB_ROWS, D_MODEL)
        else:
            x = x_ref[rws, :]
        if with_attn:
            x = x + jnp.dot(o_ref[rws, :], wo_ref[...], preferred_element_type=F32)
        h = _rms(x, g_ref[...]).astype(BF16)
        acc = None
        for c0, c1 in FFN_CHUNKS:
            gate = jnp.dot(h, win_ref[:, c0:c1], preferred_element_type=F32)
            up = jnp.dot(h, win_ref[:, D_FF + c0:D_FF + c1], preferred_element_type=F32)
            a = (jax.nn.silu(gate) * up).astype(BF16)
            part = jnp.dot(a, wout_ref[c0:c1, :], preferred_element_type=F32)
            acc = part if acc is None else acc + part
        y = x + 0.5 * acc
        cube = (BATCH, SUB_STEPS, -1)
        if tail == "final":
            (gf_ref,) = tail_refs
            y = _rms(y, gf_ref[...])
        elif tail == "kv":
            gin_ref, wd_ref, glat_ref = tail_refs
            ckv_ref, kr_ref = tail_outs
            kv_a = jnp.dot(_rms(y, gin_ref[...]).astype(BF16), wd_ref[...], preferred_element_type=F32)
            ckv_ref[:, steps, :] = _rms(kv_a[:, :KV_LORA], glat_ref[...]).astype(BF16).reshape(cube)
            kr_ref[:, steps, :] = kv_a[:, KV_LORA:].reshape(cube)
        elif tail == "q":
            gm_ref, wdq_ref, gq_ref = tail_refs
            (cq_ref,) = tail_outs
            c_q = jnp.dot(_rms(y, gm_ref[...]).astype(BF16), wdq_ref[...], preferred_element_type=F32)
            cq_ref[rws, :] = _rms(c_q, gq_ref[...]).astype(BF16)
        if relayout == "tb_to_bt":
            out_ref[:, steps, :] = y.reshape(cube)
        else:
            out_ref[rws, :] = y


def _ffn(x, gain, w_in, w_out, relayout=None, attn=None, tail=None, tail_args=(), cast_next=None):
    rows = BATCH * SEQ
    n_steps = rows // FFN_ROWS
    row_spec = pl.BlockSpec((FFN_ROWS, D_MODEL), lambda i: (i, 0))
    cube_spec = pl.BlockSpec((BATCH, FFN_STEPS, D_MODEL), lambda i: (0, i, 0))
    flat = jax.ShapeDtypeStruct((rows, D_MODEL), F32)
    cube = jax.ShapeDtypeStruct((BATCH, SEQ, D_MODEL), F32)
    in_spec = cube_spec if relayout == "bt_to_tb" else row_spec
    out_spec, out_shape = (cube_spec, cube) if relayout == "tb_to_bt" else (row_spec, flat)

    args, specs = [x], [in_spec]
    if attn is not None:
        o, w_o = attn
        args += [o, w_o]
        specs += [row_spec, _resident(w_o.shape)]
    args += [gain, w_in, w_out]
    specs += [_resident(gain.shape), _resident(w_in.shape), _resident(w_out.shape)]
    assert len(tail_args) == _TAIL_INPUTS[tail]
    args += list(tail_args)
    specs += [_resident(a.shape) for a in tail_args]
    out_specs, out_shapes = [out_spec], [out_shape]
    if tail == "kv":
        assert relayout == "tb_to_bt"
        out_specs += [pl.BlockSpec((BATCH, FFN_STEPS, LANES), lambda i: (0, i, 0))] * 2
        out_shapes += [jax.ShapeDtypeStruct((BATCH, SEQ, KV_LORA), BF16),
                       jax.ShapeDtypeStruct((BATCH, SEQ, LANES), F32)]
    elif tail == "q":
        out_specs.append(pl.BlockSpec((FFN_ROWS, Q_LORA), lambda i: (i, 0)))
        out_shapes.append(jax.ShapeDtypeStruct((rows, Q_LORA), BF16))
    if cast_next is not None:
        src_in, src_out, layer, half = cast_next
        for src in (src_in, src_out):
            n_rows, n_cols = src.shape[2:]
            slab = n_rows // n_steps
            assert slab * n_steps == n_rows and slab % 16 == 0
            args.append(src)
            specs.append(pl.BlockSpec((None, None, slab, n_cols), lambda i: (layer, half, i, 0)))
            out_specs.append(pl.BlockSpec((slab, n_cols), lambda i: (i, 0)))
            out_shapes.append(jax.ShapeDtypeStruct((n_rows, n_cols), BF16))
    return pl.pallas_call(
        functools.partial(_ffn_kernel, relayout=relayout, with_attn=attn is not None, tail=tail,
                          cast_next=cast_next is not None),
        grid=(n_steps,),
        in_specs=specs,
        out_specs=out_specs,
        out_shape=out_shapes,
        compiler_params=_params(1),
        name="ffn",
    )(*args)


def _s5_kernel(x_ref, g_ref, wb_ref, lam_ref, wc_ref, d_ref, wglu_ref, out_ref,
               bu_ref, state_ref):
    @pl.when(pl.program_id(0) == 0)
    def _():
        state_ref[...] = jnp.zeros_like(state_ref)

    x = x_ref[...]
    h = _rms(x, g_ref[...])
    hb = h.astype(BF16)

    def project_in(kt):
        bu_ref[kt] = jnp.dot(hb[:, kt * MXU_DIM:(kt + 1) * MXU_DIM], wb_ref[kt],
                             preferred_element_type=F32)

    def scan(kt):
        cols = [(half * S5_HALF, S5_MODES + half * S5_HALF) for half in range(2)]
        state = [(state_ref[kt, :, re0:re0 + S5_HALF], state_ref[kt, :, im0:im0 + S5_HALF])
                 for re0, im0 in cols]
        for t in range(S5_STEPS):
            r0 = t * SUBLANES
            for half, (re0, im0) in enumerate(cols):
                lam_re = lam_ref[kt, :, re0:re0 + S5_HALF]
                lam_im = lam_ref[kt, :, im0:im0 + S5_HALF]
                s_re, s_im = state[half]
                n_re = lam_re * s_re - lam_im * s_im + bu_ref[kt, r0:r0 + SUBLANES, re0:re0 + S5_HALF]
                n_im = lam_re * s_im + lam_im * s_re + bu_ref[kt, r0:r0 + SUBLANES, im0:im0 + S5_HALF]
                bu_ref[kt, r0:r0 + SUBLANES, re0:re0 + S5_HALF] = n_re
                bu_ref[kt, r0:r0 + SUBLANES, im0:im0 + S5_HALF] = n_im
                state[half] = (n_re, n_im)
        for (re0, im0), (s_re, s_im) in zip(cols, state):
            state_ref[kt, :, re0:re0 + S5_HALF] = s_re
            state_ref[kt, :, im0:im0 + S5_HALF] = s_im

    ys = []
    project_in(0)
    for kt in range(S5_KT):
        if kt + 1 < S5_KT:
            project_in(kt + 1)
        scan(kt)
        ys.append(jnp.dot(bu_ref[kt].astype(BF16), wc_ref[kt], preferred_element_type=F32))
    y = jnp.concatenate(ys, axis=1) + d_ref[...] * h
    y = jax.nn.gelu(y, approximate=True).astype(BF16)
    z = jnp.dot(y, wglu_ref[...], preferred_element_type=F32)
    out_ref[...] = x + z[:, :D_MODEL] * jax.nn.sigmoid(z[:, D_MODEL:])


def _s5(x, gain, wb, lam, wc, d_skip, w_glu):
    rows = x.shape[0]
    row_spec = pl.BlockSpec((S5_ROWS, D_MODEL), lambda i: (i, 0))
    return pl.pallas_call(
        _s5_kernel,
        grid=(rows // S5_ROWS,),
        in_specs=[row_spec, _resident(gain.shape), _resident(wb.shape), _resident(lam.shape),
                  _resident(wc.shape), _resident(d_skip.shape), _resident(w_glu.shape)],
        out_specs=row_spec,
        out_shape=jax.ShapeDtypeStruct((rows, D_MODEL), F32),
        scratch_shapes=[pltpu.VMEM((S5_KT, S5_ROWS, 2 * S5_MODES), F32),
                        pltpu.VMEM((S5_KT, SUBLANES, 2 * S5_MODES), F32)],
        compiler_params=_params(1),
        name="s5_mixer",
    )(x, gain, wb, lam, wc, d_skip, w_glu)


def _s5_pack_kernel(lam_gc_ref, step_gc_ref, b_ref, c_ref, lam_row_ref, step_row_ref,
                    wb_ref, wc_ref, lam_tile_ref):
    def lam_bar(lam_re, lam_im, log_step):
        step = jnp.exp(log_step)
        mag = jnp.exp(lam_re * step)
        return mag * jnp.cos(lam_im * step), mag * jnp.sin(lam_im * step)

    def spread(x, width):
        k = x.shape[1]
        rep = (lax.broadcasted_iota(jnp.int32, (k, width), 1) % k
               == lax.broadcasted_iota(jnp.int32, (k, width), 0)).astype(F32)
        return jnp.dot(x, rep, preferred_element_type=F32)

    lbar_re, lbar_im = lam_bar(lam_row_ref[0], lam_row_ref[1], step_row_ref[...])
    tile = (S5_KT, SUBLANES, S5_MODES)
    lam_tile_ref[:, :, :S5_MODES] = jnp.broadcast_to(lbar_re[:, None, :], tile)
    lam_tile_ref[:, :, S5_MODES:] = jnp.broadcast_to(lbar_im[:, None, :], tile)

    lam_re, lam_im = lam_gc_ref[0], lam_gc_ref[1]
    lb_re, lb_im = lam_bar(lam_re, lam_im, step_gc_ref[...])
    n_re, n_im = lb_re - 1.0, lb_im
    den = lam_re * lam_re + lam_im * lam_im
    f_re = (n_re * lam_re + n_im * lam_im) / den
    f_im = (n_im * lam_re - n_re * lam_im) / den
    b_re, b_im = b_ref[0], b_ref[1]
    shape = (S5_GROUPS * S5_GROUP, S5_MODES)
    own = (lax.broadcasted_iota(jnp.int32, shape, 0) // S5_GROUP % S5_GPT
           == lax.broadcasted_iota(jnp.int32, shape, 1) // S5_STATE)
    for part, bbar in enumerate((f_re * b_re - f_im * b_im, f_re * b_im + f_im * b_re)):
        wb_ref[:, part * S5_MODES:(part + 1) * S5_MODES] = jnp.where(
            own, spread(bbar, S5_MODES), 0.0).astype(BF16)

    shape = (S5_GROUPS * S5_STATE, MXU_DIM)
    own = (lax.broadcasted_iota(jnp.int32, shape, 0) // S5_STATE % S5_GPT
           == lax.broadcasted_iota(jnp.int32, shape, 1) // S5_GROUP)
    for part, sign in enumerate((1.0, -1.0)):
        packed = jnp.where(own, sign * spread(c_ref[part], MXU_DIM), 0.0).astype(BF16)
        for kt in range(S5_KT):
            wc_ref[kt, part] = packed[kt * S5_MODES:(kt + 1) * S5_MODES]


def _s5_weights(lam_re, lam_im, log_step, b_re, b_im, c_re, c_im):
    lam = jnp.stack([lam_re, lam_im]).astype(F32)
    step = log_step.astype(F32)
    lam_gc = jnp.repeat(lam, S5_GROUP, axis=1)
    step_gc = jnp.repeat(step, S5_GROUP).reshape(S5_GROUPS * S5_GROUP, 1)
    b_gc = jnp.stack([b_re, b_im]).astype(F32).transpose(0, 1, 3, 2)
    b_gc = b_gc.reshape(2, S5_GROUPS * S5_GROUP, S5_STATE)
    c_gp = jnp.stack([c_re, c_im]).astype(F32).transpose(0, 1, 3, 2)
    c_gp = c_gp.reshape(2, S5_GROUPS * S5_STATE, S5_GROUP)
    lam_row = lam.reshape(2, S5_KT, S5_MODES)
    step_row = jnp.repeat(step, S5_STATE).reshape(S5_KT, S5_MODES)
    wb, wc, lam_tile = pl.pallas_call(
        _s5_pack_kernel,
        out_shape=[jax.ShapeDtypeStruct((S5_GROUPS * S5_GROUP, 2 * S5_MODES), BF16),
                   jax.ShapeDtypeStruct((S5_KT, 2, S5_MODES, MXU_DIM), BF16),
                   jax.ShapeDtypeStruct((S5_KT, SUBLANES, 2 * S5_MODES), F32)],
        compiler_params=pltpu.CompilerParams(vmem_limit_bytes=VMEM_LIMIT_BYTES),
        name="s5_pack",
    )(lam_gc, step_gc, b_gc, c_gp, lam_row, step_row)
    return (wb.reshape(S5_KT, MXU_DIM, 2 * S5_MODES), lam_tile,
            wc.reshape(S5_KT, 2 * S5_MODES, MXU_DIM))


def _attn_kernel(cq_ref, ckv_ref, kr_ref, rot_ref, wq_ref, wkv_ref, o_ref):
    scale = math.log2(math.e) / math.sqrt(NOPE_DIM + ROPE_DIM)
    rot = rot_ref[...]
    low_lanes = lax.broadcasted_iota(jnp.int32, (SEQ, LANES), 1) < ROPE_DIM

    def rope(parts):
        t = parts * rot
        return jnp.where(low_lanes, t + pltpu.roll(t, ROPE_DIM, axis=1), 0.0)

    k_rope_t = rope(kr_ref[...]).T.astype(BF16)
    ones = jnp.ones((SEQ, V_DIM), BF16)
    row_chunk = lax.broadcasted_iota(jnp.int32, (ATT_Q, ATT_Q), 0) // CHUNK
    col_chunk = lax.broadcasted_iota(jnp.int32, (ATT_Q, ATT_Q), 1) // CHUNK
    diag_mask = col_chunk <= row_chunk
    neg = jnp.finfo(F32).min

    def project(hd):
        q = jnp.dot(cq_ref[...], wq_ref[hd], preferred_element_type=F32)
        q = jnp.concatenate([(q[:, :NOPE_DIM] * scale).astype(BF16),
                             (rope(q[:, NOPE_DIM:]) * scale).astype(BF16)], axis=1)
        kv = jnp.dot(ckv_ref[...], wkv_ref[hd], preferred_element_type=F32)
        k_t = jnp.concatenate([kv[:, :NOPE_DIM].T.astype(BF16), k_rope_t], axis=0)
        v_ones = jnp.concatenate([kv[:, NOPE_DIM:].astype(BF16), ones], axis=1)
        return q, k_t, v_ones

    projected = project(0)
    for hd in range(ATT_HEADS):
        q, k_t, v_ones = projected
        if hd + 1 < ATT_HEADS:
            projected = project(hd + 1)

        def scores(i):
            return jnp.dot(q[i * ATT_Q:(i + 1) * ATT_Q, :], k_t[:, 0:(i + 1) * ATT_Q],
                           preferred_element_type=F32)

        order = list(reversed(range(SEQ // ATT_Q)))
        s_next = scores(order[0])
        for pos, i in enumerate(order):
            q0, n_keys = i * ATT_Q, (i + 1) * ATT_Q
            s, s_next = s_next, (scores(order[pos + 1]) if pos + 1 < len(order) else None)
            s_diag = jnp.where(diag_mask, s[:, q0:n_keys], neg)
            s = s_diag if i == 0 else jnp.concatenate([s[:, :q0], s_diag], axis=1)
            m = jnp.max(s, axis=-1, keepdims=True)
            p = jnp.exp2(s - m).astype(BF16)
            o = jnp.dot(p, v_ones[0:n_keys, :], preferred_element_type=F32)
            o_ref[q0:q0 + ATT_Q, hd * V_DIM:(hd + 1) * V_DIM] = (
                o[:, :V_DIM] / o[:, V_DIM:]).astype(BF16)


def _attention(c_q, c_kv, k_parts, rot_table, wq, wkv):
    def per_batch(width):
        return pl.BlockSpec((None, SEQ, width), lambda b, h: (b, 0, 0))

    def per_head_group(w):
        return pl.BlockSpec((ATT_HEADS,) + w.shape[1:], lambda b, h: (h, 0, 0))

    return pl.pallas_call(
        _attn_kernel,
        grid=(BATCH, MLA_HEADS // ATT_HEADS),
        in_specs=[per_batch(Q_LORA), per_batch(KV_LORA), per_batch(LANES), per_batch(LANES),
                  per_head_group(wq), per_head_group(wkv)],
        out_specs=pl.BlockSpec((None, SEQ, ATT_HEADS * V_DIM), lambda b, h: (b, 0, h)),
        out_shape=jax.ShapeDtypeStruct((BATCH, SEQ, MLA_HEADS * V_DIM), BF16),
        compiler_params=_params(2),
        name="mla_attention",
    )(c_q, c_kv, k_parts, rot_table, wq, wkv)


def _rope_table(positions):
    inv_freq = 1.0 / (ROPE_BASE ** (jnp.arange(0, ROPE_DIM, 2, dtype=F32) / ROPE_DIM))
    ang = positions.astype(F32)[..., None] * inv_freq
    cos, sin = jnp.cos(ang), jnp.sin(ang)
    return jnp.concatenate([cos, cos, -sin, sin], axis=-1)


def _with_rot_cols(w):
    half = ROPE_DIM // 2
    return jnp.concatenate([w, w[..., half:], w[..., :half]], axis=-1)


def kernel(x, positions, norm_gains, ffn_w_in, ffn_w_out, s5_lambda_re, s5_lambda_im, s5_log_step, s5_b_re, s5_b_im, s5_c_re, s5_c_im, s5_d, s5_w_glu, kv_in_norm, w_dkv, kv_latent_norm, w_ukv, mla_w_dq, mla_q_norm, mla_w_uq, mla_w_o, final_norm):
    assert x.shape == (BATCH, SEQ, D_MODEL) and norm_gains.shape[0] == 2
    rows = BATCH * SEQ
    gains = norm_gains.astype(F32).reshape(2, 3, 1, D_MODEL)
    w_in_f32, w_out_f32 = ffn_w_in.astype(F32), ffn_w_out.astype(F32)

    xt, w_in, w_out = _ffn(x.astype(F32), gains[0, 0], w_in_f32[0, 0].astype(BF16),
                           w_out_f32[0, 0].astype(BF16), relayout="bt_to_tb",
                           cast_next=(w_in_f32, w_out_f32, 0, 1))
    wb, lam_tile, wc = _s5_weights(s5_lambda_re[0], s5_lambda_im[0], s5_log_step[0],
                                   s5_b_re[0], s5_b_im[0], s5_c_re[0], s5_c_im[0])
    xt = _s5(xt, gains[0, 1], wb, lam_tile, wc, s5_d[0].astype(F32).reshape(1, D_MODEL),
             s5_w_glu[0].astype(BF16))
    wd = jnp.concatenate([w_dkv[:, :KV_LORA], _with_rot_cols(w_dkv[:, KV_LORA:])], axis=-1)
    xb, c_kv, k_parts, w_in, w_out = _ffn(
        xt, gains[0, 2], w_in, w_out, relayout="tb_to_bt", tail="kv",
        tail_args=(kv_in_norm.astype(F32).reshape(1, D_MODEL), wd.astype(BF16),
                   kv_latent_norm.astype(F32).reshape(1, KV_LORA)),
        cast_next=(w_in_f32, w_out_f32, 1, 0))
    xb = xb.reshape(rows, D_MODEL)

    xb, c_q, w_in, w_out = _ffn(xb, gains[1, 0], w_in, w_out, tail="q",
                                tail_args=(gains[1, 1], mla_w_dq[0].astype(BF16),
                                           mla_q_norm[0].astype(F32).reshape(1, Q_LORA)),
                                cast_next=(w_in_f32, w_out_f32, 1, 1))
    w_uq_h = mla_w_uq[0].reshape(Q_LORA, MLA_HEADS, NOPE_DIM + ROPE_DIM)
    wq = jnp.concatenate([w_uq_h[..., :NOPE_DIM], _with_rot_cols(w_uq_h[..., NOPE_DIM:])], axis=-1)
    wq = wq.transpose(1, 0, 2).astype(BF16)
    wkv = w_ukv.reshape(KV_LORA, MLA_HEADS, NOPE_DIM + V_DIM).transpose(1, 0, 2).astype(BF16)
    o = _attention(c_q.reshape(BATCH, SEQ, Q_LORA), c_kv, k_parts, _rope_table(positions), wq, wkv)
    (out,) = _ffn(xb, gains[1, 2], w_in, w_out,
                  attn=(o.reshape(rows, MLA_HEADS * V_DIM), mla_w_o[0].astype(BF16)),
                  tail="final", tail_args=(final_norm.astype(F32).reshape(1, D_MODEL),))
    return out.reshape(BATCH, SEQ, D_MODEL).astype(x.dtype)
```

```python
import functools
import math

import jax
import jax.numpy as jnp
from jax import lax
from jax.experimental import pallas as pl
from jax.experimental.pallas import tpu as pltpu

D_MODEL = 1024
BATCH = 8
SEQ = 2048
CHUNK = 64
EPS = 1e-6
D_FF = 2816
S5_GROUP = 16
S5_GROUPS = D_MODEL // S5_GROUP
S5_STATE = 64
MLA_HEADS = 8
NOPE_DIM = 128
ROPE_DIM = 64
V_DIM = 128
Q_LORA = 256
KV_LORA = 128
ROPE_BASE = 10000.0

LANES = 128
SUBLANES = 8
MXU_DIM = 256
VMEM_LIMIT_BYTES = 56 * 1024 * 1024

HEAD_PAD = 2 * LANES
FFN_STEPS = 128
FFN_ROWS = FFN_STEPS * BATCH
FFN_SUBTILES = 2
SUB_STEPS = FFN_STEPS // FFN_SUBTILES
SUB_ROWS = FFN_ROWS // FFN_SUBTILES
FFN_CHUNKS = ((0, 1024), (1024, 2048), (2048, 2816))
S5_STEPS = 64
S5_ROWS = S5_STEPS * BATCH
S5_KT = D_MODEL // MXU_DIM
S5_GPT = MXU_DIM // S5_GROUP
S5_MODES = S5_GPT * S5_STATE
S5_HALF = S5_MODES // 2
ATT_Q = 256
ATT_HEADS = 2

BF16 = jnp.bfloat16
F32 = jnp.float32


def _rms(x, gain):
    return x * lax.rsqrt(jnp.mean(x * x, axis=-1, keepdims=True) + EPS) * gain


def _resident(shape):
    zeros = (0,) * len(shape)
    return pl.BlockSpec(shape, lambda *_: zeros, pipeline_mode=pl.Buffered(1))


def _params(n_grid_dims):
    return pltpu.CompilerParams(
        dimension_semantics=("arbitrary",) * n_grid_dims,
        vmem_limit_bytes=VMEM_LIMIT_BYTES)


_TAIL_INPUTS = {None: 0, "final": 1, "kv": 3, "q": 3}
_TAIL_OUTPUTS = {None: 0, "final": 0, "kv": 2, "q": 1}


def _ffn_kernel(*refs, relayout, with_attn, tail, cast_next):
    refs = list(refs)
    x_ref = refs.pop(0)
    if with_attn:
        o_ref, wo_ref = refs.pop(0), refs.pop(0)
    g_ref, win_ref, wout_ref = refs.pop(0), refs.pop(0), refs.pop(0)
    tail_refs = [refs.pop(0) for _ in range(_TAIL_INPUTS[tail])]
    cast_srcs = [refs.pop(0) for _ in range(2 if cast_next else 0)]
    out_ref = refs.pop(0)
    tail_outs = [refs.pop(0) for _ in range(_TAIL_OUTPUTS[tail])]
    for src_ref, dst_ref in zip(cast_srcs, refs, strict=True):
        dst_ref[...] = src_ref[...].astype(BF16)

    def slices(sub):
        return (slice(sub * SUB_STEPS, (sub + 1) * SUB_STEPS),
                slice(sub * SUB_ROWS, (sub + 1) * SUB_ROWS))

    def head(sub):
        steps, rws = slices(sub)
        if relayout == "bt_to_tb":
            x = jnp.swapaxes(x_ref[:, steps, :], 0, 1).reshape(SUB_ROWS, D_MODEL)
        elif relayout == "tb_to_bt":
            x = jnp.swapaxes(x_ref[rws, :].reshape(SUB_STEPS, BATCH, D_MODEL), 0, 1)
            x = x.reshape(SUB_ROWS, D_MODEL)
        else:
            x = x_ref[rws, :]
        if with_attn:
            x = x + jnp.dot(o_ref[rws, :], wo_ref[...], preferred_element_type=F32)
        return x, _rms(x, g_ref[...]).astype(BF16)

    def chunk(h, acc, c0, c1):
        gate = jnp.dot(h, win_ref[:, c0:c1], preferred_element_type=F32)
        up = jnp.dot(h, win_ref[:, D_FF + c0:D_FF + c1], preferred_element_type=F32)
        a = (jax.nn.silu(gate) * up).astype(BF16)
        part = jnp.dot(a, wout_ref[c0:c1, :], preferred_element_type=F32)
        return part if acc is None else acc + part

    def finish(sub, x, acc):
        steps, rws = slices(sub)
        y = x + 0.5 * acc
        cube = (BATCH, SUB_STEPS, -1)
        if tail == "final":
            (gf_ref,) = tail_refs
            y = _rms(y, gf_ref[...])
        elif tail == "kv":
            gin_ref, wd_ref, glat_ref = tail_refs
            ckv_ref, kr_ref = tail_outs
            kv_a = jnp.dot(_rms(y, gin_ref[...]).astype(BF16), wd_ref[...], preferred_element_type=F32)
            ckv_ref[:, steps, :] = _rms(kv_a[:, :KV_LORA], glat_ref[...]).astype(BF16).reshape(cube)
            kr_ref[:, steps, :] = kv_a[:, KV_LORA:].reshape(cube)
        elif tail == "q":
            gm_ref, wdq_ref, gq_ref = tail_refs
            (cq_ref,) = tail_outs
            c_q = jnp.dot(_rms(y, gm_ref[...]).astype(BF16), wdq_ref[...], preferred_element_type=F32)
            cq_ref[rws, :] = _rms(c_q, gq_ref[...]).astype(BF16)
        if relayout == "tb_to_bt":
            out_ref[:, steps, :] = y.reshape(cube)
        else:
            out_ref[rws, :] = y

    for sub in range(FFN_SUBTILES):
        x, h = head(sub)
        acc = None
        for c0, c1 in FFN_CHUNKS:
            acc = chunk(h, acc, c0, c1)
        finish(sub, x, acc)


def _ffn(x, gain, w_in, w_out, relayout=None, attn=None, tail=None, tail_args=(), cast_next=None):
    rows = BATCH * SEQ
    n_steps = rows // FFN_ROWS
    row_spec = pl.BlockSpec((FFN_ROWS, D_MODEL), lambda i: (i, 0))
    cube_spec = pl.BlockSpec((BATCH, FFN_STEPS, D_MODEL), lambda i: (0, i, 0))
    flat = jax.ShapeDtypeStruct((rows, D_MODEL), F32)
    cube = jax.ShapeDtypeStruct((BATCH, SEQ, D_MODEL), F32)
    in_spec = cube_spec if relayout == "bt_to_tb" else row_spec
    out_spec, out_shape = (cube_spec, cube) if relayout == "tb_to_bt" else (row_spec, flat)

    args, specs = [x], [in_spec]
    if attn is not None:
        o, w_o = attn
        args += [o, w_o]
        specs += [row_spec, _resident(w_o.shape)]
    args += [gain, w_in, w_out]
    specs += [_resident(gain.shape), _resident(w_in.shape), _resident(w_out.shape)]
    assert len(tail_args) == _TAIL_INPUTS[tail]
    args += list(tail_args)
    specs += [_resident(a.shape) for a in tail_args]
    out_specs, out_shapes = [out_spec], [out_shape]
    if tail == "kv":
        assert relayout == "tb_to_bt"
        out_specs += [pl.BlockSpec((BATCH, FFN_STEPS, LANES), lambda i: (0, i, 0))] * 2
        out_shapes += [jax.ShapeDtypeStruct((BATCH, SEQ, KV_LORA), BF16),
                       jax.ShapeDtypeStruct((BATCH, SEQ, LANES), F32)]
    elif tail == "q":
        out_specs.append(pl.BlockSpec((FFN_ROWS, Q_LORA), lambda i: (i, 0)))
        out_shapes.append(jax.ShapeDtypeStruct((rows, Q_LORA), BF16))
    if cast_next is not None:
        src_in, src_out, layer, half = cast_next
        for src in (src_in, src_out):
            n_rows, n_cols = src.shape[2:]
            slab = n_rows // n_steps
            assert slab * n_steps == n_rows and slab % 16 == 0
            args.append(src)
            specs.append(pl.BlockSpec((None, None, slab, n_cols), lambda i: (layer, half, i, 0)))
            out_specs.append(pl.BlockSpec((slab, n_cols), lambda i: (i, 0)))
            out_shapes.append(jax.ShapeDtypeStruct((n_rows, n_cols), BF16))
    return pl.pallas_call(
        functools.partial(_ffn_kernel, relayout=relayout, with_attn=attn is not None, tail=tail,
                          cast_next=cast_next is not None),
        grid=(n_steps,),
        in_specs=specs,
        out_specs=out_specs,
        out_shape=out_shapes,
        compiler_params=_params(1),
        name="ffn",
    )(*args)


def _s5_kernel(x_ref, g_ref, wb_ref, lam_ref, wc_ref, d_ref, wglu_ref, out_ref,
               bu_ref, state_ref):
    @pl.when(pl.program_id(0) == 0)
    def _():
        state_ref[...] = jnp.zeros_like(state_ref)

    x = x_ref[...]
    h = _rms(x, g_ref[...])
    hb = h.astype(BF16)

    def project_in(kt):
        bu_ref[kt] = jnp.dot(hb[:, kt * MXU_DIM:(kt + 1) * MXU_DIM], wb_ref[kt],
                             preferred_element_type=F32)

    def scan(kt):
        for half in range(2):
            re0, im0 = half * S5_HALF, S5_MODES + half * S5_HALF
            lam_re = lam_ref[kt, :, re0:re0 + S5_HALF]
            lam_im = lam_ref[kt, :, im0:im0 + S5_HALF]
            s_re = state_ref[kt, :, re0:re0 + S5_HALF]
            s_im = state_ref[kt, :, im0:im0 + S5_HALF]
            for t in range(S5_STEPS):
                r0 = t * SUBLANES
                n_re = lam_re * s_re - lam_im * s_im + bu_ref[kt, r0:r0 + SUBLANES, re0:re0 + S5_HALF]
                n_im = lam_re * s_im + lam_im * s_re + bu_ref[kt, r0:r0 + SUBLANES, im0:im0 + S5_HALF]
                bu_ref[kt, r0:r0 + SUBLANES, re0:re0 + S5_HALF] = n_re
                bu_ref[kt, r0:r0 + SUBLANES, im0:im0 + S5_HALF] = n_im
                s_re, s_im = n_re, n_im
            state_ref[kt, :, re0:re0 + S5_HALF] = s_re
            state_ref[kt, :, im0:im0 + S5_HALF] = s_im

    ys = []
    project_in(0)
    for kt in range(S5_KT):
        if kt + 1 < S5_KT:
            project_in(kt + 1)
        scan(kt)
        ys.append(jnp.dot(bu_ref[kt].astype(BF16), wc_ref[kt], preferred_element_type=F32))
    y = jnp.concatenate(ys, axis=1) + d_ref[...] * h
    y = jax.nn.gelu(y, approximate=True).astype(BF16)
    z = jnp.dot(y, wglu_ref[...], preferred_element_type=F32)
    out_ref[...] = x + z[:, :D_MODEL] * jax.nn.sigmoid(z[:, D_MODEL:])


def _s5(x, gain, wb, lam, wc, d_skip, w_glu):
    rows = x.shape[0]
    row_spec = pl.BlockSpec((S5_ROWS, D_MODEL), lambda i: (i, 0))
    return pl.pallas_call(
        _s5_kernel,
        grid=(rows // S5_ROWS,),
        in_specs=[row_spec, _resident(gain.shape), _resident(wb.shape), _resident(lam.shape),
                  _resident(wc.shape), _resident(d_skip.shape), _resident(w_glu.shape)],
        out_specs=row_spec,
        out_shape=jax.ShapeDtypeStruct((rows, D_MODEL), F32),
        scratch_shapes=[pltpu.VMEM((S5_KT, S5_ROWS, 2 * S5_MODES), F32),
                        pltpu.VMEM((S5_KT, SUBLANES, 2 * S5_MODES), F32)],
        compiler_params=_params(1),
        name="s5_mixer",
    )(x, gain, wb, lam, wc, d_skip, w_glu)


def _s5_pack_kernel(lam_gc_ref, step_gc_ref, b_ref, c_ref, lam_row_ref, step_row_ref,
                    wb_ref, wc_ref, lam_tile_ref):
    def lam_bar(lam_re, lam_im, log_step):
        step = jnp.exp(log_step)
        mag = jnp.exp(lam_re * step)
        return mag * jnp.cos(lam_im * step), mag * jnp.sin(lam_im * step)

    def spread(x, width):
        k = x.shape[1]
        rep = (lax.broadcasted_iota(jnp.int32, (k, width), 1) % k
               == lax.broadcasted_iota(jnp.int32, (k, width), 0)).astype(F32)
        return jnp.dot(x, rep, preferred_element_type=F32)

    lbar_re, lbar_im = lam_bar(lam_row_ref[0], lam_row_ref[1], step_row_ref[...])
    tile = (S5_KT, SUBLANES, S5_MODES)
    lam_tile_ref[:, :, :S5_MODES] = jnp.broadcast_to(lbar_re[:, None, :], tile)
    lam_tile_ref[:, :, S5_MODES:] = jnp.broadcast_to(lbar_im[:, None, :], tile)

    lam_re, lam_im = lam_gc_ref[0], lam_gc_ref[1]
    lb_re, lb_im = lam_bar(lam_re, lam_im, step_gc_ref[...])
    n_re, n_im = lb_re - 1.0, lb_im
    den = lam_re * lam_re + lam_im * lam_im
    f_re = (n_re * lam_re + n_im * lam_im) / den
    f_im = (n_im * lam_re - n_re * lam_im) / den
    b_re, b_im = b_ref[0], b_ref[1]
    shape = (S5_GROUPS * S5_GROUP, S5_MODES)
    own = (lax.broadcasted_iota(jnp.int32, shape, 0) // S5_GROUP % S5_GPT
           == lax.broadcasted_iota(jnp.int32, shape, 1) // S5_STATE)
    for part, bbar in enumerate((f_re * b_re - f_im * b_im, f_re * b_im + f_im * b_re)):
        wb_ref[:, part * S5_MODES:(part + 1) * S5_MODES] = jnp.where(
            own, spread(bbar, S5_MODES), 0.0).astype(BF16)

    shape = (S5_GROUPS * S5_STATE, MXU_DIM)
    own = (lax.broadcasted_iota(jnp.int32, shape, 0) // S5_STATE % S5_GPT
           == lax.broadcasted_iota(jnp.int32, shape, 1) // S5_GROUP)
    for part, sign in enumerate((1.0, -1.0)):
        packed = jnp.where(own, sign * spread(c_ref[part], MXU_DIM), 0.0).astype(BF16)
        for kt in range(S5_KT):
            wc_ref[kt, part] = packed[kt * S5_MODES:(kt + 1) * S5_MODES]


def _s5_weights(lam_re, lam_im, log_step, b_re, b_im, c_re, c_im):
    lam = jnp.stack([lam_re, lam_im]).astype(F32)
    step = log_step.astype(F32)
    lam_gc = jnp.repeat(lam, S5_GROUP, axis=1)
    step_gc = jnp.repeat(step, S5_GROUP).reshape(S5_GROUPS * S5_GROUP, 1)
    b_gc = jnp.stack([b_re, b_im]).astype(F32).transpose(0, 1, 3, 2)
    b_gc = b_gc.reshape(2, S5_GROUPS * S5_GROUP, S5_STATE)
    c_gp = jnp.stack([c_re, c_im]).astype(F32).transpose(0, 1, 3, 2)
    c_gp = c_gp.reshape(2, S5_GROUPS * S5_STATE, S5_GROUP)
    lam_row = lam.reshape(2, S5_KT, S5_MODES)
    step_row = jnp.repeat(step, S5_STATE).reshape(S5_KT, S5_MODES)
    wb, wc, lam_tile = pl.pallas_call(
        _s5_pack_kernel,
        out_shape=[jax.ShapeDtypeStruct((S5_GROUPS * S5_GROUP, 2 * S5_MODES), BF16),
                   jax.ShapeDtypeStruct((S5_KT, 2, S5_MODES, MXU_DIM), BF16),
                   jax.ShapeDtypeStruct((S5_KT, SUBLANES, 2 * S5_MODES), F32)],
        compiler_params=pltpu.CompilerParams(vmem_limit_bytes=VMEM_LIMIT_BYTES),
        name="s5_pack",
    )(lam_gc, step_gc, b_gc, c_gp, lam_row, step_row)
    return (wb.reshape(S5_KT, MXU_DIM, 2 * S5_MODES), lam_tile,
            wc.reshape(S5_KT, 2 * S5_MODES, MXU_DIM))


def _attn_kernel(cq_ref, ckv_ref, kr_ref, rot_ref, wq_ref, wkv_ref, o_ref):
    scale = math.log2(math.e) / math.sqrt(NOPE_DIM + ROPE_DIM)
    rot = rot_ref[...].T
    low_lanes = lax.broadcasted_iota(jnp.int32, (SEQ, LANES), 1) < ROPE_DIM

    def rope(parts):
        t = parts * rot
        return jnp.where(low_lanes, t + pltpu.roll(t, ROPE_DIM, axis=1), 0.0)

    k_rope_t = rope(kr_ref[...]).T.astype(BF16)
    ones = jnp.ones((SEQ, V_DIM), BF16)
    row_chunk = lax.broadcasted_iota(jnp.int32, (ATT_Q, ATT_Q), 0) // CHUNK
    col_chunk = lax.broadcasted_iota(jnp.int32, (ATT_Q, ATT_Q), 1) // CHUNK
    diag_mask = col_chunk <= row_chunk
    neg = jnp.finfo(F32).min

    def project(hd):
        q = jnp.dot(cq_ref[...], wq_ref[hd], preferred_element_type=F32)
        q = jnp.concatenate([(q[:, :NOPE_DIM] * scale).astype(BF16),
                             (rope(q[:, NOPE_DIM:]) * scale).astype(BF16)], axis=1)
        kv = jnp.dot(ckv_ref[...], wkv_ref[hd], preferred_element_type=F32)
        k_t = jnp.concatenate([kv[:, :NOPE_DIM].T.astype(BF16), k_rope_t], axis=0)
        v_ones = jnp.concatenate([kv[:, NOPE_DIM:].astype(BF16), ones], axis=1)
        return q, k_t, v_ones

    projected = project(0)
    for hd in range(ATT_HEADS):
        q, k_t, v_ones = projected
        if hd + 1 < ATT_HEADS:
            projected = project(hd + 1)

        def scores(i):
            return jnp.dot(q[i * ATT_Q:(i + 1) * ATT_Q, :], k_t[:, 0:(i + 1) * ATT_Q],
                           preferred_element_type=F32)

        order = list(reversed(range(SEQ // ATT_Q)))
        s_next = scores(order[0])
        for pos, i in enumerate(order):
            q0, n_keys = i * ATT_Q, (i + 1) * ATT_Q
            s, s_next = s_next, (scores(order[pos + 1]) if pos + 1 < len(order) else None)
            s_diag = jnp.where(diag_mask, s[:, q0:n_keys], neg)
            s = s_diag if i == 0 else jnp.concatenate([s[:, :q0], s_diag], axis=1)
            m = jnp.max(s, axis=-1, keepdims=True)
            p = jnp.exp2(s - m).astype(BF16)
            o = jnp.dot(p, v_ones[0:n_keys, :], preferred_element_type=F32)
            o_ref[q0:q0 + ATT_Q, hd * V_DIM:(hd + 1) * V_DIM] = (
                o[:, :V_DIM] / o[:, V_DIM:]).astype(BF16)


def _attention(c_q, c_kv, k_parts, rot_table, wq, wkv):
    def per_batch(width):
        return pl.BlockSpec((None, SEQ, width), lambda b, h: (b, 0, 0))

    def per_head_group(w):
        return pl.BlockSpec((ATT_HEADS,) + w.shape[1:], lambda b, h: (h, 0, 0))

    return pl.pallas_call(
        _attn_kernel,
        grid=(BATCH, MLA_HEADS // ATT_HEADS),
        in_specs=[per_batch(Q_LORA), per_batch(KV_LORA), per_batch(LANES),
                  pl.BlockSpec((2 * ROPE_DIM, SEQ), lambda b, h: (0, b)),
                  per_head_group(wq), per_head_group(wkv)],
        out_specs=pl.BlockSpec((None, SEQ, ATT_HEADS * V_DIM), lambda b, h: (b, 0, h)),
        out_shape=jax.ShapeDtypeStruct((BATCH, SEQ, MLA_HEADS * V_DIM), BF16),
        compiler_params=_params(2),
        name="mla_attention",
    )(c_q, c_kv, k_parts, rot_table, wq, wkv)


def _rope_kernel(inv_freq_ref, pos_ref, out_ref):
    half = ROPE_DIM // 2
    ang = pos_ref[...].astype(F32) * inv_freq_ref[...]
    cos, sin = jnp.cos(ang), jnp.sin(ang)
    out_ref[0:half] = cos
    out_ref[half:2 * half] = cos
    out_ref[2 * half:3 * half] = -sin
    out_ref[3 * half:4 * half] = sin


def _rope_table(positions):
    inv_freq = 1.0 / (ROPE_BASE ** (jnp.arange(0, ROPE_DIM, 2, dtype=F32) / ROPE_DIM))
    return pl.pallas_call(
        _rope_kernel,
        out_shape=jax.ShapeDtypeStruct((2 * ROPE_DIM, positions.size), F32),
        compiler_params=pltpu.CompilerParams(vmem_limit_bytes=VMEM_LIMIT_BYTES),
        name="rope_table",
    )(inv_freq.reshape(ROPE_DIM // 2, 1), positions.reshape(1, positions.size))


def _with_rot_cols(w):
    half = ROPE_DIM // 2
    return jnp.concatenate([w, w[..., half:], w[..., :half]], axis=-1)


def kernel(x, positions, norm_gains, ffn_w_in, ffn_w_out, s5_lambda_re, s5_lambda_im, s5_log_step, s5_b_re, s5_b_im, s5_c_re, s5_c_im, s5_d, s5_w_glu, kv_in_norm, w_dkv, kv_latent_norm, w_ukv, mla_w_dq, mla_q_norm, mla_w_uq, mla_w_o, final_norm):
    assert x.shape == (BATCH, SEQ, D_MODEL) and norm_gains.shape[0] == 2
    rows = BATCH * SEQ
    gains = norm_gains.astype(F32).reshape(2, 3, 1, D_MODEL)
    w_in_f32, w_out_f32 = ffn_w_in.astype(F32), ffn_w_out.astype(F32)

    xt, w_in, w_out = _ffn(x.astype(F32), gains[0, 0], w_in_f32[0, 0].astype(BF16),
                           w_out_f32[0, 0].astype(BF16), relayout="bt_to_tb",
                           cast_next=(w_in_f32, w_out_f32, 0, 1))
    wb, lam_tile, wc = _s5_weights(s5_lambda_re[0], s5_lambda_im[0], s5_log_step[0],
                                   s5_b_re[0], s5_b_im[0], s5_c_re[0], s5_c_im[0])
    xt = _s5(xt, gains[0, 1], wb, lam_tile, wc, s5_d[0].astype(F32).reshape(1, D_MODEL),
             s5_w_glu[0].astype(BF16))
    wd = jnp.concatenate([w_dkv[:, :KV_LORA], _with_rot_cols(w_dkv[:, KV_LORA:])], axis=-1)
    xb, c_kv, k_parts, w_in, w_out = _ffn(
        xt, gains[0, 2], w_in, w_out, relayout="tb_to_bt", tail="kv",
        tail_args=(kv_in_norm.astype(F32).reshape(1, D_MODEL), wd.astype(BF16),
                   kv_latent_norm.astype(F32).reshape(1, KV_LORA)),
        cast_next=(w_in_f32, w_out_f32, 1, 0))
    xb = xb.reshape(rows, D_MODEL)

    xb, c_q, w_in, w_out = _ffn(xb, gains[1, 0], w_in, w_out, tail="q",
                                tail_args=(gains[1, 1], mla_w_dq[0].astype(BF16),
                                           mla_q_norm[0].astype(F32).reshape(1, Q_LORA)),
                                cast_next=(w_in_f32, w_out_f32, 1, 1))
    w_uq_h = mla_w_uq[0].reshape(Q_LORA, MLA_HEADS, NOPE_DIM + ROPE_DIM)
    wq = jnp.concatenate([w_uq_h[..., :NOPE_DIM], _with_rot_cols(w_uq_h[..., NOPE_DIM:])], axis=-1)
    wq = wq.transpose(1, 0, 2).astype(BF16)
    wkv = w_ukv.reshape(KV_LORA, MLA_HEADS, NOPE_DIM + V_DIM).transpose(1, 0, 2).astype(BF16)
    o = _attention(c_q.reshape(BATCH, SEQ, Q_LORA), c_kv, k_parts, _rope_table(positions), wq, wkv)
    (out,) = _ffn(xb, gains[1, 2], w_in, w_out,
                  attn=(o.reshape(rows, MLA_HEADS * V_DIM), mla_w_o[0].astype(BF16)),
                  tail="final", tail_args=(final_norm.astype(F32).reshape(1, D_MODEL),))
    return out.reshape(BATCH, SEQ, D_MODEL).astype(x.dtype)
```

```python
import functools
import math

import jax
import jax.numpy as jnp
from jax import lax
from jax.experimental import pallas as pl
from jax.experimental.pallas import tpu as pltpu

D_MODEL = 1024
BATCH = 8
SEQ = 2048
CHUNK = 64
EPS = 1e-6
D_FF = 2816
S5_GROUP = 16
S5_GROUPS = D_MODEL // S5_GROUP
S5_STATE = 64
MLA_HEADS = 8
NOPE_DIM = 128
ROPE_DIM = 64
V_DIM = 128
Q_LORA = 256
KV_LORA = 128
ROPE_BASE = 10000.0

LANES = 128
SUBLANES = 8
MXU_DIM = 256
VMEM_LIMIT_BYTES = 56 * 1024 * 1024

HEAD_PAD = 2 * LANES
FFN_STEPS = 128
FFN_ROWS = FFN_STEPS * BATCH
FFN_SUBTILES = 2
SUB_STEPS = FFN_STEPS // FFN_SUBTILES
SUB_ROWS = FFN_ROWS // FFN_SUBTILES
FFN_CHUNKS = ((0, 1024), (1024, 2048), (2048, 2816))
S5_STEPS = 64
S5_ROWS = S5_STEPS * BATCH
S5_KT = D_MODEL // MXU_DIM
S5_GPT = MXU_DIM // S5_GROUP
S5_MODES = S5_GPT * S5_STATE
S5_HALF = S5_MODES // 2
ATT_Q = 256
ATT_HEADS = 4

BF16 = jnp.bfloat16
F32 = jnp.float32


def _rms(x, gain):
    return x * lax.rsqrt(jnp.mean(x * x, axis=-1, keepdims=True) + EPS) * gain


def _resident(shape):
    zeros = (0,) * len(shape)
    return pl.BlockSpec(shape, lambda *_: zeros, pipeline_mode=pl.Buffered(1))


def _params(n_grid_dims):
    return pltpu.CompilerParams(
        dimension_semantics=("arbitrary",) * n_grid_dims,
        vmem_limit_bytes=VMEM_LIMIT_BYTES)


_TAIL_INPUTS = {None: 0, "final": 1, "kv": 3, "q": 3}
_TAIL_OUTPUTS = {None: 0, "final": 0, "kv": 2, "q": 1}


def _ffn_kernel(*refs, relayout, with_attn, tail, cast_next):
    refs = list(refs)
    x_ref = refs.pop(0)
    if with_attn:
        o_ref, wo_ref = refs.pop(0), refs.pop(0)
    g_ref, win_ref, wout_ref = refs.pop(0), refs.pop(0), refs.pop(0)
    tail_refs = [refs.pop(0) for _ in range(_TAIL_INPUTS[tail])]
    cast_srcs = [refs.pop(0) for _ in range(2 if cast_next else 0)]
    out_ref = refs.pop(0)
    tail_outs = [refs.pop(0) for _ in range(_TAIL_OUTPUTS[tail])]
    for src_ref, dst_ref in zip(cast_srcs, refs, strict=True):
        dst_ref[...] = src_ref[...].astype(BF16)

    def slices(sub):
        return (slice(sub * SUB_STEPS, (sub + 1) * SUB_STEPS),
                slice(sub * SUB_ROWS, (sub + 1) * SUB_ROWS))

    def head(sub):
        steps, rws = slices(sub)
        if relayout == "bt_to_tb":
            x = jnp.swapaxes(x_ref[:, steps, :], 0, 1).reshape(SUB_ROWS, D_MODEL)
        elif relayout == "tb_to_bt":
            x = jnp.swapaxes(x_ref[rws, :].reshape(SUB_STEPS, BATCH, D_MODEL), 0, 1)
            x = x.reshape(SUB_ROWS, D_MODEL)
        else:
            x = x_ref[rws, :]
        if with_attn:
            x = x + jnp.dot(o_ref[rws, :], wo_ref[...], preferred_element_type=F32)
        return x, _rms(x, g_ref[...]).astype(BF16)

    def chunk(h, acc, c0, c1):
        gate = jnp.dot(h, win_ref[:, c0:c1], preferred_element_type=F32)
        up = jnp.dot(h, win_ref[:, D_FF + c0:D_FF + c1], preferred_element_type=F32)
        a = (jax.nn.silu(gate) * up).astype(BF16)
        part = jnp.dot(a, wout_ref[c0:c1, :], preferred_element_type=F32)
        return part if acc is None else acc + part

    def finish(sub, x, acc):
        steps, rws = slices(sub)
        y = x + 0.5 * acc
        cube = (BATCH, SUB_STEPS, -1)
        if tail == "final":
            (gf_ref,) = tail_refs
            y = _rms(y, gf_ref[...])
        elif tail == "kv":
            gin_ref, wd_ref, glat_ref = tail_refs
            ckv_ref, kr_ref = tail_outs
            kv_a = jnp.dot(_rms(y, gin_ref[...]).astype(BF16), wd_ref[...], preferred_element_type=F32)
            ckv_ref[:, steps, :] = _rms(kv_a[:, :KV_LORA], glat_ref[...]).astype(BF16).reshape(cube)
            kr_ref[:, steps, :] = kv_a[:, KV_LORA:].reshape(cube)
        elif tail == "q":
            gm_ref, wdq_ref, gq_ref = tail_refs
            (cq_ref,) = tail_outs
            c_q = jnp.dot(_rms(y, gm_ref[...]).astype(BF16), wdq_ref[...], preferred_element_type=F32)
            cq_ref[rws, :] = _rms(c_q, gq_ref[...]).astype(BF16)
        if relayout == "tb_to_bt":
            out_ref[:, steps, :] = y.reshape(cube)
        else:
            out_ref[rws, :] = y

    for sub in range(FFN_SUBTILES):
        x, h = head(sub)
        acc = None
        for c0, c1 in FFN_CHUNKS:
            acc = chunk(h, acc, c0, c1)
        finish(sub, x, acc)


def _ffn(x, gain, w_in, w_out, relayout=None, attn=None, tail=None, tail_args=(), cast_next=None):
    rows = BATCH * SEQ
    n_steps = rows // FFN_ROWS
    row_spec = pl.BlockSpec((FFN_ROWS, D_MODEL), lambda i: (i, 0))
    cube_spec = pl.BlockSpec((BATCH, FFN_STEPS, D_MODEL), lambda i: (0, i, 0))
    flat = jax.ShapeDtypeStruct((rows, D_MODEL), F32)
    cube = jax.ShapeDtypeStruct((BATCH, SEQ, D_MODEL), F32)
    in_spec = cube_spec if relayout == "bt_to_tb" else row_spec
    out_spec, out_shape = (cube_spec, cube) if relayout == "tb_to_bt" else (row_spec, flat)

    args, specs = [x], [in_spec]
    if attn is not None:
        o, w_o = attn
        args += [o, w_o]
        specs += [row_spec, _resident(w_o.shape)]
    args += [gain, w_in, w_out]
    specs += [_resident(gain.shape), _resident(w_in.shape), _resident(w_out.shape)]
    assert len(tail_args) == _TAIL_INPUTS[tail]
    args += list(tail_args)
    specs += [_resident(a.shape) for a in tail_args]
    out_specs, out_shapes = [out_spec], [out_shape]
    if tail == "kv":
        assert relayout == "tb_to_bt"
        out_specs += [pl.BlockSpec((BATCH, FFN_STEPS, LANES), lambda i: (0, i, 0))] * 2
        out_shapes += [jax.ShapeDtypeStruct((BATCH, SEQ, KV_LORA), BF16),
                       jax.ShapeDtypeStruct((BATCH, SEQ, LANES), F32)]
    elif tail == "q":
        out_specs.append(pl.BlockSpec((FFN_ROWS, Q_LORA), lambda i: (i, 0)))
        out_shapes.append(jax.ShapeDtypeStruct((rows, Q_LORA), BF16))
    if cast_next is not None:
        src_in, src_out, layer, half = cast_next
        for src in (src_in, src_out):
            n_rows, n_cols = src.shape[2:]
            slab = n_rows // n_steps
            assert slab * n_steps == n_rows and slab % 16 == 0
            args.append(src)
            specs.append(pl.BlockSpec((None, None, slab, n_cols), lambda i: (layer, half, i, 0)))
            out_specs.append(pl.BlockSpec((slab, n_cols), lambda i: (i, 0)))
            out_shapes.append(jax.ShapeDtypeStruct((n_rows, n_cols), BF16))
    return pl.pallas_call(
        functools.partial(_ffn_kernel, relayout=relayout, with_attn=attn is not None, tail=tail,
                          cast_next=cast_next is not None),
        grid=(n_steps,),
        in_specs=specs,
        out_specs=out_specs,
        out_shape=out_shapes,
        compiler_params=_params(1),
        name="ffn",
    )(*args)


def _s5_kernel(x_ref, g_ref, wb_ref, lam_ref, wc_ref, d_ref, wglu_ref, out_ref,
               bu_ref, state_ref):
    @pl.when(pl.program_id(0) == 0)
    def _():
        state_ref[...] = jnp.zeros_like(state_ref)

    x = x_ref[...]
    h = _rms(x, g_ref[...])
    hb = h.astype(BF16)

    def project_in(kt):
        bu_ref[kt] = jnp.dot(hb[:, kt * MXU_DIM:(kt + 1) * MXU_DIM], wb_ref[kt],
                             preferred_element_type=F32)

    def scan(kt):
        for half in range(2):
            re0, im0 = half * S5_HALF, S5_MODES + half * S5_HALF
            lam_re = lam_ref[kt, :, re0:re0 + S5_HALF]
            lam_im = lam_ref[kt, :, im0:im0 + S5_HALF]
            s_re = state_ref[kt, :, re0:re0 + S5_HALF]
            s_im = state_ref[kt, :, im0:im0 + S5_HALF]
            for t in range(S5_STEPS):
                r0 = t * SUBLANES
                n_re = lam_re * s_re - lam_im * s_im + bu_ref[kt, r0:r0 + SUBLANES, re0:re0 + S5_HALF]
                n_im = lam_re * s_im + lam_im * s_re + bu_ref[kt, r0:r0 + SUBLANES, im0:im0 + S5_HALF]
                bu_ref[kt, r0:r0 + SUBLANES, re0:re0 + S5_HALF] = n_re
                bu_ref[kt, r0:r0 + SUBLANES, im0:im0 + S5_HALF] = n_im
                s_re, s_im = n_re, n_im
            state_ref[kt, :, re0:re0 + S5_HALF] = s_re
            state_ref[kt, :, im0:im0 + S5_HALF] = s_im

    ys = []
    project_in(0)
    for kt in range(S5_KT):
        if kt + 1 < S5_KT:
            project_in(kt + 1)
        scan(kt)
        ys.append(jnp.dot(bu_ref[kt].astype(BF16), wc_ref[kt], preferred_element_type=F32))
    y = jnp.concatenate(ys, axis=1) + d_ref[...] * h
    y = jax.nn.gelu(y, approximate=True).astype(BF16)
    z = jnp.dot(y, wglu_ref[...], preferred_element_type=F32)
    out_ref[...] = x + z[:, :D_MODEL] * jax.nn.sigmoid(z[:, D_MODEL:])


def _s5(x, gain, wb, lam, wc, d_skip, w_glu):
    rows = x.shape[0]
    row_spec = pl.BlockSpec((S5_ROWS, D_MODEL), lambda i: (i, 0))
    return pl.pallas_call(
        _s5_kernel,
        grid=(rows // S5_ROWS,),
        in_specs=[row_spec, _resident(gain.shape), _resident(wb.shape), _resident(lam.shape),
                  _resident(wc.shape), _resident(d_skip.shape), _resident(w_glu.shape)],
        out_specs=row_spec,
        out_shape=jax.ShapeDtypeStruct((rows, D_MODEL), F32),
        scratch_shapes=[pltpu.VMEM((S5_KT, S5_ROWS, 2 * S5_MODES), F32),
                        pltpu.VMEM((S5_KT, SUBLANES, 2 * S5_MODES), F32)],
        compiler_params=_params(1),
        name="s5_mixer",
    )(x, gain, wb, lam, wc, d_skip, w_glu)


def _s5_pack_kernel(lam_gc_ref, step_gc_ref, b_ref, c_ref, lam_row_ref, step_row_ref,
                    wb_ref, wc_ref, lam_tile_ref):
    def lam_bar(lam_re, lam_im, log_step):
        step = jnp.exp(log_step)
        mag = jnp.exp(lam_re * step)
        return mag * jnp.cos(lam_im * step), mag * jnp.sin(lam_im * step)

    def spread(x, width):
        k = x.shape[1]
        rep = (lax.broadcasted_iota(jnp.int32, (k, width), 1) % k
               == lax.broadcasted_iota(jnp.int32, (k, width), 0)).astype(F32)
        return jnp.dot(x, rep, preferred_element_type=F32)

    lbar_re, lbar_im = lam_bar(lam_row_ref[0], lam_row_ref[1], step_row_ref[...])
    tile = (S5_KT, SUBLANES, S5_MODES)
    lam_tile_ref[:, :, :S5_MODES] = jnp.broadcast_to(lbar_re[:, None, :], tile)
    lam_tile_ref[:, :, S5_MODES:] = jnp.broadcast_to(lbar_im[:, None, :], tile)

    lam_re, lam_im = lam_gc_ref[0], lam_gc_ref[1]
    lb_re, lb_im = lam_bar(lam_re, lam_im, step_gc_ref[...])
    n_re, n_im = lb_re - 1.0, lb_im
    den = lam_re * lam_re + lam_im * lam_im
    f_re = (n_re * lam_re + n_im * lam_im) / den
    f_im = (n_im * lam_re - n_re * lam_im) / den
    b_re, b_im = b_ref[0], b_ref[1]
    shape = (S5_GROUPS * S5_GROUP, S5_MODES)
    own = (lax.broadcasted_iota(jnp.int32, shape, 0) // S5_GROUP % S5_GPT
           == lax.broadcasted_iota(jnp.int32, shape, 1) // S5_STATE)
    for part, bbar in enumerate((f_re * b_re - f_im * b_im, f_re * b_im + f_im * b_re)):
        wb_ref[:, part * S5_MODES:(part + 1) * S5_MODES] = jnp.where(
            own, spread(bbar, S5_MODES), 0.0).astype(BF16)

    shape = (S5_GROUPS * S5_STATE, MXU_DIM)
    own = (lax.broadcasted_iota(jnp.int32, shape, 0) // S5_STATE % S5_GPT
           == lax.broadcasted_iota(jnp.int32, shape, 1) // S5_GROUP)
    for part, sign in enumerate((1.0, -1.0)):
        packed = jnp.where(own, sign * spread(c_ref[part], MXU_DIM), 0.0).astype(BF16)
        for kt in range(S5_KT):
            wc_ref[kt, part] = packed[kt * S5_MODES:(kt + 1) * S5_MODES]


def _s5_weights(lam_re, lam_im, log_step, b_re, b_im, c_re, c_im):
    lam = jnp.stack([lam_re, lam_im]).astype(F32)
    step = log_step.astype(F32)
    lam_gc = jnp.repeat(lam, S5_GROUP, axis=1)
    step_gc = jnp.repeat(step, S5_GROUP).reshape(S5_GROUPS * S5_GROUP, 1)
    b_gc = jnp.stack([b_re, b_im]).astype(F32).transpose(0, 1, 3, 2)
    b_gc = b_gc.reshape(2, S5_GROUPS * S5_GROUP, S5_STATE)
    c_gp = jnp.stack([c_re, c_im]).astype(F32).transpose(0, 1, 3, 2)
    c_gp = c_gp.reshape(2, S5_GROUPS * S5_STATE, S5_GROUP)
    lam_row = lam.reshape(2, S5_KT, S5_MODES)
    step_row = jnp.repeat(step, S5_STATE).reshape(S5_KT, S5_MODES)
    wb, wc, lam_tile = pl.pallas_call(
        _s5_pack_kernel,
        out_shape=[jax.ShapeDtypeStruct((S5_GROUPS * S5_GROUP, 2 * S5_MODES), BF16),
                   jax.ShapeDtypeStruct((S5_KT, 2, S5_MODES, MXU_DIM), BF16),
                   jax.ShapeDtypeStruct((S5_KT, SUBLANES, 2 * S5_MODES), F32)],
        compiler_params=pltpu.CompilerParams(vmem_limit_bytes=VMEM_LIMIT_BYTES),
        name="s5_pack",
    )(lam_gc, step_gc, b_gc, c_gp, lam_row, step_row)
    return (wb.reshape(S5_KT, MXU_DIM, 2 * S5_MODES), lam_tile,
            wc.reshape(S5_KT, 2 * S5_MODES, MXU_DIM))


def _attn_kernel(cq_ref, ckv_ref, kr_ref, rot_ref, wq_ref, wkv_ref, o_ref):
    scale = math.log2(math.e) / math.sqrt(NOPE_DIM + ROPE_DIM)
    rot = rot_ref[...].T
    low_lanes = lax.broadcasted_iota(jnp.int32, (SEQ, LANES), 1) < ROPE_DIM

    def rope(parts):
        t = parts * rot
        return jnp.where(low_lanes, t + pltpu.roll(t, ROPE_DIM, axis=1), 0.0)

    k_rope_t = rope(kr_ref[...]).T.astype(BF16)
    ones = jnp.ones((SEQ, V_DIM), BF16)
    row_chunk = lax.broadcasted_iota(jnp.int32, (ATT_Q, ATT_Q), 0) // CHUNK
    col_chunk = lax.broadcasted_iota(jnp.int32, (ATT_Q, ATT_Q), 1) // CHUNK
    diag_mask = col_chunk <= row_chunk
    neg = jnp.finfo(F32).min

    def project(hd):
        q = jnp.dot(cq_ref[...], wq_ref[hd], preferred_element_type=F32)
        q = jnp.concatenate([(q[:, :NOPE_DIM] * scale).astype(BF16),
                             (rope(q[:, NOPE_DIM:]) * scale).astype(BF16)], axis=1)
        kv = jnp.dot(ckv_ref[...], wkv_ref[hd], preferred_element_type=F32)
        k_t = jnp.concatenate([kv[:, :NOPE_DIM].T.astype(BF16), k_rope_t], axis=0)
        v_ones = jnp.concatenate([kv[:, NOPE_DIM:].astype(BF16), ones], axis=1)
        return q, k_t, v_ones

    projected = project(0)
    for hd in range(ATT_HEADS):
        q, k_t, v_ones = projected
        if hd + 1 < ATT_HEADS:
            projected = project(hd + 1)

        def scores(i):
            return jnp.dot(q[i * ATT_Q:(i + 1) * ATT_Q, :], k_t[:, 0:(i + 1) * ATT_Q],
                           preferred_element_type=F32)

        order = list(reversed(range(SEQ // ATT_Q)))
        s_next = scores(order[0])
        for pos, i in enumerate(order):
            q0, n_keys = i * ATT_Q, (i + 1) * ATT_Q
            s, s_next = s_next, (scores(order[pos + 1]) if pos + 1 < len(order) else None)
            s_diag = jnp.where(diag_mask, s[:, q0:n_keys], neg)
            s = s_diag if i == 0 else jnp.concatenate([s[:, :q0], s_diag], axis=1)
            m = jnp.max(s, axis=-1, keepdims=True)
            p = jnp.exp2(s - m).astype(BF16)
            o = jnp.dot(p, v_ones[0:n_keys, :], preferred_element_type=F32)
            o_ref[q0:q0 + ATT_Q, hd * V_DIM:(hd + 1) * V_DIM] = (
                o[:, :V_DIM] / o[:, V_DIM:]).astype(BF16)


def _attention(c_q, c_kv, k_parts, rot_table, wq, wkv):
    def per_batch(width):
        return pl.BlockSpec((None, SEQ, width), lambda b, h: (b, 0, 0))

    def per_head_group(w):
        return pl.BlockSpec((ATT_HEADS,) + w.shape[1:], lambda b, h: (h, 0, 0))

    return pl.pallas_call(
        _attn_kernel,
        grid=(BATCH, MLA_HEADS // ATT_HEADS),
        in_specs=[per_batch(Q_LORA), per_batch(KV_LORA), per_batch(LANES),
                  pl.BlockSpec((2 * ROPE_DIM, SEQ), lambda b, h: (0, b)),
                  per_head_group(wq), per_head_group(wkv)],
        out_specs=pl.BlockSpec((None, SEQ, ATT_HEADS * V_DIM), lambda b, h: (b, 0, h)),
        out_shape=jax.ShapeDtypeStruct((BATCH, SEQ, MLA_HEADS * V_DIM), BF16),
        compiler_params=_params(2),
        name="mla_attention",
    )(c_q, c_kv, k_parts, rot_table, wq, wkv)


def _rope_kernel(inv_freq_ref, pos_ref, out_ref):
    half = ROPE_DIM // 2
    ang = pos_ref[...].astype(F32) * inv_freq_ref[...]
    cos, sin = jnp.cos(ang), jnp.sin(ang)
    out_ref[0:half] = cos
    out_ref[half:2 * half] = cos
    out_ref[2 * half:3 * half] = -sin
    out_ref[3 * half:4 * half] = sin


def _rope_table(positions):
    inv_freq = 1.0 / (ROPE_BASE ** (jnp.arange(0, ROPE_DIM, 2, dtype=F32) / ROPE_DIM))
    return pl.pallas_call(
        _rope_kernel,
        out_shape=jax.ShapeDtypeStruct((2 * ROPE_DIM, positions.size), F32),
        compiler_params=pltpu.CompilerParams(vmem_limit_bytes=VMEM_LIMIT_BYTES),
        name="rope_table",
    )(inv_freq.reshape(ROPE_DIM // 2, 1), positions.reshape(1, positions.size))


def _with_rot_cols(w):
    half = ROPE_DIM // 2
    return jnp.concatenate([w, w[..., half:], w[..., :half]], axis=-1)


def kernel(x, positions, norm_gains, ffn_w_in, ffn_w_out, s5_lambda_re, s5_lambda_im, s5_log_step, s5_b_re, s5_b_im, s5_c_re, s5_c_im, s5_d, s5_w_glu, kv_in_norm, w_dkv, kv_latent_norm, w_ukv, mla_w_dq, mla_q_norm, mla_w_uq, mla_w_o, final_norm):
    assert x.shape == (BATCH, SEQ, D_MODEL) and norm_gains.shape[0] == 2
    rows = BATCH * SEQ
    gains = norm_gains.astype(F32).reshape(2, 3, 1, D_MODEL)
    w_in_f32, w_out_f32 = ffn_w_in.astype(F32), ffn_w_out.astype(F32)

    xt, w_in, w_out = _ffn(x.astype(F32), gains[0, 0], w_in_f32[0, 0].astype(BF16),
                           w_out_f32[0, 0].astype(BF16), relayout="bt_to_tb",
                           cast_next=(w_in_f32, w_out_f32, 0, 1))
    wb, lam_tile, wc = _s5_weights(s5_lambda_re[0], s5_lambda_im[0], s5_log_step[0],
                                   s5_b_re[0], s5_b_im[0], s5_c_re[0], s5_c_im[0])
    xt = _s5(xt, gains[0, 1], wb, lam_tile, wc, s5_d[0].astype(F32).reshape(1, D_MODEL),
             s5_w_glu[0].astype(BF16))
    wd = jnp.concatenate([w_dkv[:, :KV_LORA], _with_rot_cols(w_dkv[:, KV_LORA:])], axis=-1)
    xb, c_kv, k_parts, w_in, w_out = _ffn(
        xt, gains[0, 2], w_in, w_out, relayout="tb_to_bt", tail="kv",
        tail_args=(kv_in_norm.astype(F32).reshape(1, D_MODEL), wd.astype(BF16),
                   kv_latent_norm.astype(F32).reshape(1, KV_LORA)),
        cast_next=(w_in_f32, w_out_f32, 1, 0))
    xb = xb.reshape(rows, D_MODEL)

    xb, c_q, w_in, w_out = _ffn(xb, gains[1, 0], w_in, w_out, tail="q",
                                tail_args=(gains[1, 1], mla_w_dq[0].astype(BF16),
                                           mla_q_norm[0].astype(F32).reshape(1, Q_LORA)),
                                cast_next=(w_in_f32, w_out_f32, 1, 1))
    w_uq_h = mla_w_uq[0].reshape(Q_LORA, MLA_HEADS, NOPE_DIM + ROPE_DIM)
    wq = jnp.concatenate([w_uq_h[..., :NOPE_DIM], _with_rot_cols(w_uq_h[..., NOPE_DIM:])], axis=-1)
    wq = wq.transpose(1, 0, 2).astype(BF16)
    wkv = w_ukv.reshape(KV_LORA, MLA_HEADS, NOPE_DIM + V_DIM).transpose(1, 0, 2).astype(BF16)
    o = _attention(c_q.reshape(BATCH, SEQ, Q_LORA), c_kv, k_parts, _rope_table(positions), wq, wkv)
    (out,) = _ffn(xb, gains[1, 2], w_in, w_out,
                  attn=(o.reshape(rows, MLA_HEADS * V_DIM), mla_w_o[0].astype(BF16)),
                  tail="final", tail_args=(final_norm.astype(F32).reshape(1, D_MODEL),))
    return out.reshape(BATCH, SEQ, D_MODEL).astype(x.dtype)
```

```python
import functools
import math

import jax
import jax.numpy as jnp
from jax import lax
from jax.experimental import pallas as pl
from jax.experimental.pallas import tpu as pltpu

D_MODEL = 1024
BATCH = 8
SEQ = 2048
CHUNK = 64
EPS = 1e-6
D_FF = 2816
S5_GROUP = 16
S5_GROUPS = D_MODEL // S5_GROUP
S5_STATE = 64
MLA_HEADS = 8
NOPE_DIM = 128
ROPE_DIM = 64
V_DIM = 128
Q_LORA = 256
KV_LORA = 128
ROPE_BASE = 10000.0

LANES = 128
SUBLANES = 8
MXU_DIM = 256
VMEM_LIMIT_BYTES = 56 * 1024 * 1024

HEAD_PAD = 2 * LANES
FFN_STEPS = 128
FFN_ROWS = FFN_STEPS * BATCH
FFN_SUBTILES = 2
SUB_STEPS = FFN_STEPS // FFN_SUBTILES
SUB_ROWS = FFN_ROWS // FFN_SUBTILES
FFN_CHUNKS = ((0, 1536), (1536, 2816))
S5_STEPS = 64
S5_ROWS = S5_STEPS * BATCH
S5_KT = D_MODEL // MXU_DIM
S5_GPT = MXU_DIM // S5_GROUP
S5_MODES = S5_GPT * S5_STATE
S5_HALF = S5_MODES // 2
ATT_Q = 256
ATT_HEADS = 4

BF16 = jnp.bfloat16
F32 = jnp.float32


def _rms(x, gain):
    return x * lax.rsqrt(jnp.mean(x * x, axis=-1, keepdims=True) + EPS) * gain


def _resident(shape):
    zeros = (0,) * len(shape)
    return pl.BlockSpec(shape, lambda *_: zeros, pipeline_mode=pl.Buffered(1))


def _params(n_grid_dims):
    return pltpu.CompilerParams(
        dimension_semantics=("arbitrary",) * n_grid_dims,
        vmem_limit_bytes=VMEM_LIMIT_BYTES)


_TAIL_INPUTS = {None: 0, "final": 1, "kv": 3, "q": 3}
_TAIL_OUTPUTS = {None: 0, "final": 0, "kv": 2, "q": 1}


def _ffn_kernel(*refs, relayout, with_attn, tail, n_casts):
    refs = list(refs)
    x_ref = refs.pop(0)
    if with_attn:
        o_ref, wo_ref = refs.pop(0), refs.pop(0)
    g_ref, win_ref, wout_ref = refs.pop(0), refs.pop(0), refs.pop(0)
    tail_refs = [refs.pop(0) for _ in range(_TAIL_INPUTS[tail])]
    cast_srcs = [refs.pop(0) for _ in range(n_casts)]
    out_ref = refs.pop(0)
    tail_outs = [refs.pop(0) for _ in range(_TAIL_OUTPUTS[tail])]
    for src_ref, dst_ref in zip(cast_srcs, refs, strict=True):
        dst_ref[...] = src_ref[...].astype(BF16)

    def slices(sub):
        return (slice(sub * SUB_STEPS, (sub + 1) * SUB_STEPS),
                slice(sub * SUB_ROWS, (sub + 1) * SUB_ROWS))

    def head(sub):
        steps, rws = slices(sub)
        if relayout == "bt_to_tb":
            x = jnp.swapaxes(x_ref[:, steps, :], 0, 1).reshape(SUB_ROWS, D_MODEL)
        elif relayout == "tb_to_bt":
            x = jnp.swapaxes(x_ref[rws, :].reshape(SUB_STEPS, BATCH, D_MODEL), 0, 1)
            x = x.reshape(SUB_ROWS, D_MODEL)
        else:
            x = x_ref[rws, :]
        if with_attn:
            x = x + jnp.dot(o_ref[rws, :], wo_ref[...], preferred_element_type=F32)
        return x, _rms(x, g_ref[...]).astype(BF16)

    def chunk(h, acc, c0, c1):
        gate = jnp.dot(h, win_ref[:, c0:c1], preferred_element_type=F32)
        up = jnp.dot(h, win_ref[:, D_FF + c0:D_FF + c1], preferred_element_type=F32)
        a = (jax.nn.silu(gate) * up).astype(BF16)
        part = jnp.dot(a, wout_ref[c0:c1, :], preferred_element_type=F32)
        return part if acc is None else acc + part

    def finish(sub, x, acc):
        steps, rws = slices(sub)
        y = x + 0.5 * acc
        cube = (BATCH, SUB_STEPS, -1)
        if tail == "final":
            (gf_ref,) = tail_refs
            y = _rms(y, gf_ref[...])
        elif tail == "kv":
            gin_ref, wd_ref, glat_ref = tail_refs
            ckv_ref, kr_ref = tail_outs
            kv_a = jnp.dot(_rms(y, gin_ref[...]).astype(BF16), wd_ref[...], preferred_element_type=F32)
            ckv_ref[:, steps, :] = _rms(kv_a[:, :KV_LORA], glat_ref[...]).astype(BF16).reshape(cube)
            kr_ref[:, steps, :] = kv_a[:, KV_LORA:].reshape(cube)
        elif tail == "q":
            gm_ref, wdq_ref, gq_ref = tail_refs
            (cq_ref,) = tail_outs
            c_q = jnp.dot(_rms(y, gm_ref[...]).astype(BF16), wdq_ref[...], preferred_element_type=F32)
            cq_ref[rws, :] = _rms(c_q, gq_ref[...]).astype(BF16)
        if relayout == "tb_to_bt":
            out_ref[:, steps, :] = y.reshape(cube)
        else:
            out_ref[rws, :] = y

    for sub in range(FFN_SUBTILES):
        x, h = head(sub)
        acc = None
        for c0, c1 in FFN_CHUNKS:
            acc = chunk(h, acc, c0, c1)
        finish(sub, x, acc)


def _ffn(x, gain, w_in, w_out, relayout=None, attn=None, tail=None, tail_args=(), cast_next=()):
    rows = BATCH * SEQ
    n_steps = rows // FFN_ROWS
    row_spec = pl.BlockSpec((FFN_ROWS, D_MODEL), lambda i: (i, 0))
    cube_spec = pl.BlockSpec((BATCH, FFN_STEPS, D_MODEL), lambda i: (0, i, 0))
    flat = jax.ShapeDtypeStruct((rows, D_MODEL), F32)
    cube = jax.ShapeDtypeStruct((BATCH, SEQ, D_MODEL), F32)
    in_spec = cube_spec if relayout == "bt_to_tb" else row_spec
    out_spec, out_shape = (cube_spec, cube) if relayout == "tb_to_bt" else (row_spec, flat)

    args, specs = [x], [in_spec]
    if attn is not None:
        o, w_o = attn
        args += [o, w_o]
        specs += [row_spec, _resident(w_o.shape)]
    args += [gain, w_in, w_out]
    specs += [_resident(gain.shape), _resident(w_in.shape), _resident(w_out.shape)]
    assert len(tail_args) == _TAIL_INPUTS[tail]
    args += list(tail_args)
    specs += [_resident(a.shape) for a in tail_args]
    out_specs, out_shapes = [out_spec], [out_shape]
    if tail == "kv":
        assert relayout == "tb_to_bt"
        out_specs += [pl.BlockSpec((BATCH, FFN_STEPS, LANES), lambda i: (0, i, 0))] * 2
        out_shapes += [jax.ShapeDtypeStruct((BATCH, SEQ, KV_LORA), BF16),
                       jax.ShapeDtypeStruct((BATCH, SEQ, LANES), F32)]
    elif tail == "q":
        out_specs.append(pl.BlockSpec((FFN_ROWS, Q_LORA), lambda i: (i, 0)))
        out_shapes.append(jax.ShapeDtypeStruct((rows, Q_LORA), BF16))
    for src, lead in cast_next:
        n_rows, n_cols = src.shape[len(lead):]
        slab = n_rows // n_steps
        assert slab * n_steps == n_rows and slab % 16 == 0
        args.append(src)
        specs.append(pl.BlockSpec((None,) * len(lead) + (slab, n_cols),
                                  lambda i, lead=lead: lead + (i, 0)))
        out_specs.append(pl.BlockSpec((slab, n_cols), lambda i: (i, 0)))
        out_shapes.append(jax.ShapeDtypeStruct((n_rows, n_cols), BF16))
    return pl.pallas_call(
        functools.partial(_ffn_kernel, relayout=relayout, with_attn=attn is not None, tail=tail,
                          n_casts=len(cast_next)),
        grid=(n_steps,),
        in_specs=specs,
        out_specs=out_specs,
        out_shape=out_shapes,
        compiler_params=_params(1),
        name="ffn",
    )(*args)


def _s5_kernel(x_ref, g_ref, wb_ref, lam_ref, wc_ref, d_ref, wglu_ref, out_ref,
               bu_ref, state_ref):
    @pl.when(pl.program_id(0) == 0)
    def _():
        state_ref[...] = jnp.zeros_like(state_ref)

    x = x_ref[...]
    h = _rms(x, g_ref[...])
    hb = h.astype(BF16)

    def project_in(kt):
        bu_ref[kt] = jnp.dot(hb[:, kt * MXU_DIM:(kt + 1) * MXU_DIM], wb_ref[kt],
                             preferred_element_type=F32)

    def scan(kt):
        for half in range(2):
            re0, im0 = half * S5_HALF, S5_MODES + half * S5_HALF
            lam_re = lam_ref[kt, :, re0:re0 + S5_HALF]
            lam_im = lam_ref[kt, :, im0:im0 + S5_HALF]
            s_re = state_ref[kt, :, re0:re0 + S5_HALF]
            s_im = state_ref[kt, :, im0:im0 + S5_HALF]
            for t in range(S5_STEPS):
                r0 = t * SUBLANES
                n_re = lam_re * s_re - lam_im * s_im + bu_ref[kt, r0:r0 + SUBLANES, re0:re0 + S5_HALF]
                n_im = lam_re * s_im + lam_im * s_re + bu_ref[kt, r0:r0 + SUBLANES, im0:im0 + S5_HALF]
                bu_ref[kt, r0:r0 + SUBLANES, re0:re0 + S5_HALF] = n_re
                bu_ref[kt, r0:r0 + SUBLANES, im0:im0 + S5_HALF] = n_im
                s_re, s_im = n_re, n_im
            state_ref[kt, :, re0:re0 + S5_HALF] = s_re
            state_ref[kt, :, im0:im0 + S5_HALF] = s_im

    ys = []
    project_in(0)
    for kt in range(S5_KT):
        if kt + 1 < S5_KT:
            project_in(kt + 1)
        scan(kt)
        ys.append(jnp.dot(bu_ref[kt].astype(BF16), wc_ref[kt], preferred_element_type=F32))
    y = jnp.concatenate(ys, axis=1) + d_ref[...] * h
    y = jax.nn.gelu(y, approximate=True).astype(BF16)
    z = jnp.dot(y, wglu_ref[...], preferred_element_type=F32)
    out_ref[...] = x + z[:, :D_MODEL] * jax.nn.sigmoid(z[:, D_MODEL:])


def _s5(x, gain, wb, lam, wc, d_skip, w_glu):
    rows = x.shape[0]
    row_spec = pl.BlockSpec((S5_ROWS, D_MODEL), lambda i: (i, 0))
    return pl.pallas_call(
        _s5_kernel,
        grid=(rows // S5_ROWS,),
        in_specs=[row_spec, _resident(gain.shape), _resident(wb.shape), _resident(lam.shape),
                  _resident(wc.shape), _resident(d_skip.shape), _resident(w_glu.shape)],
        out_specs=row_spec,
        out_shape=jax.ShapeDtypeStruct((rows, D_MODEL), F32),
        scratch_shapes=[pltpu.VMEM((S5_KT, S5_ROWS, 2 * S5_MODES), F32),
                        pltpu.VMEM((S5_KT, SUBLANES, 2 * S5_MODES), F32)],
        compiler_params=_params(1),
        name="s5_mixer",
    )(x, gain, wb, lam, wc, d_skip, w_glu)


def _s5_pack_kernel(lam_gc_ref, step_gc_ref, b_ref, c_ref, lam_row_ref, step_row_ref,
                    wb_ref, wc_ref, lam_tile_ref):
    def lam_bar(lam_re, lam_im, log_step):
        step = jnp.exp(log_step)
        mag = jnp.exp(lam_re * step)
        return mag * jnp.cos(lam_im * step), mag * jnp.sin(lam_im * step)

    def spread(x, width):
        k = x.shape[1]
        rep = (lax.broadcasted_iota(jnp.int32, (k, width), 1) % k
               == lax.broadcasted_iota(jnp.int32, (k, width), 0)).astype(F32)
        return jnp.dot(x, rep, preferred_element_type=F32)

    lbar_re, lbar_im = lam_bar(lam_row_ref[0], lam_row_ref[1], step_row_ref[...])
    tile = (S5_KT, SUBLANES, S5_MODES)
    lam_tile_ref[:, :, :S5_MODES] = jnp.broadcast_to(lbar_re[:, None, :], tile)
    lam_tile_ref[:, :, S5_MODES:] = jnp.broadcast_to(lbar_im[:, None, :], tile)

    lam_re, lam_im = lam_gc_ref[0], lam_gc_ref[1]
    lb_re, lb_im = lam_bar(lam_re, lam_im, step_gc_ref[...])
    n_re, n_im = lb_re - 1.0, lb_im
    den = lam_re * lam_re + lam_im * lam_im
    f_re = (n_re * lam_re + n_im * lam_im) / den
    f_im = (n_im * lam_re - n_re * lam_im) / den
    b_re, b_im = b_ref[0], b_ref[1]
    shape = (S5_GROUPS * S5_GROUP, S5_MODES)
    own = (lax.broadcasted_iota(jnp.int32, shape, 0) // S5_GROUP % S5_GPT
           == lax.broadcasted_iota(jnp.int32, shape, 1) // S5_STATE)
    for part, bbar in enumerate((f_re * b_re - f_im * b_im, f_re * b_im + f_im * b_re)):
        wb_ref[:, part * S5_MODES:(part + 1) * S5_MODES] = jnp.where(
            own, spread(bbar, S5_MODES), 0.0).astype(BF16)

    shape = (S5_GROUPS * S5_STATE, MXU_DIM)
    own = (lax.broadcasted_iota(jnp.int32, shape, 0) // S5_STATE % S5_GPT
           == lax.broadcasted_iota(jnp.int32, shape, 1) // S5_GROUP)
    for part, sign in enumerate((1.0, -1.0)):
        packed = jnp.where(own, sign * spread(c_ref[part], MXU_DIM), 0.0).astype(BF16)
        for kt in range(S5_KT):
            wc_ref[kt, part] = packed[kt * S5_MODES:(kt + 1) * S5_MODES]


def _s5_weights(lam_re, lam_im, log_step, b_re, b_im, c_re, c_im):
    lam = jnp.stack([lam_re, lam_im]).astype(F32)
    step = log_step.astype(F32)
    lam_gc = jnp.repeat(lam, S5_GROUP, axis=1)
    step_gc = jnp.repeat(step, S5_GROUP).reshape(S5_GROUPS * S5_GROUP, 1)
    b_gc = jnp.stack([b_re, b_im]).astype(F32).transpose(0, 1, 3, 2)
    b_gc = b_gc.reshape(2, S5_GROUPS * S5_GROUP, S5_STATE)
    c_gp = jnp.stack([c_re, c_im]).astype(F32).transpose(0, 1, 3, 2)
    c_gp = c_gp.reshape(2, S5_GROUPS * S5_STATE, S5_GROUP)
    lam_row = lam.reshape(2, S5_KT, S5_MODES)
    step_row = jnp.repeat(step, S5_STATE).reshape(S5_KT, S5_MODES)
    wb, wc, lam_tile = pl.pallas_call(
        _s5_pack_kernel,
        out_shape=[jax.ShapeDtypeStruct((S5_GROUPS * S5_GROUP, 2 * S5_MODES), BF16),
                   jax.ShapeDtypeStruct((S5_KT, 2, S5_MODES, MXU_DIM), BF16),
                   jax.ShapeDtypeStruct((S5_KT, SUBLANES, 2 * S5_MODES), F32)],
        compiler_params=pltpu.CompilerParams(vmem_limit_bytes=VMEM_LIMIT_BYTES),
        name="s5_pack",
    )(lam_gc, step_gc, b_gc, c_gp, lam_row, step_row)
    return (wb.reshape(S5_KT, MXU_DIM, 2 * S5_MODES), lam_tile,
            wc.reshape(S5_KT, 2 * S5_MODES, MXU_DIM))


def _attn_kernel(cq_ref, ckv_ref, kr_ref, rot_ref, wq_ref, wkv_ref, o_ref):
    scale = math.log2(math.e) / math.sqrt(NOPE_DIM + ROPE_DIM)
    rot = rot_ref[...].T
    low_lanes = lax.broadcasted_iota(jnp.int32, (SEQ, LANES), 1) < ROPE_DIM

    def rope(parts):
        t = parts * rot
        return jnp.where(low_lanes, t + pltpu.roll(t, ROPE_DIM, axis=1), 0.0)

    k_rope_t = rope(kr_ref[...]).T.astype(BF16)
    ones = jnp.ones((SEQ, V_DIM), BF16)
    row_chunk = lax.broadcasted_iota(jnp.int32, (ATT_Q, ATT_Q), 0) // CHUNK
    col_chunk = lax.broadcasted_iota(jnp.int32, (ATT_Q, ATT_Q), 1) // CHUNK
    diag_mask = col_chunk <= row_chunk
    neg = jnp.finfo(F32).min

    def project(hd):
        q = jnp.dot(cq_ref[...], wq_ref[hd], preferred_element_type=F32)
        q = jnp.concatenate([(q[:, :NOPE_DIM] * scale).astype(BF16),
                             (rope(q[:, NOPE_DIM:]) * scale).astype(BF16)], axis=1)
        kv = jnp.dot(ckv_ref[...], wkv_ref[hd], preferred_element_type=F32)
        k_t = jnp.concatenate([kv[:, :NOPE_DIM].T.astype(BF16), k_rope_t], axis=0)
        v_ones = jnp.concatenate([kv[:, NOPE_DIM:].astype(BF16), ones], axis=1)
        return q, k_t, v_ones

    projected = project(0)
    for hd in range(ATT_HEADS):
        q, k_t, v_ones = projected
        if hd + 1 < ATT_HEADS:
            projected = project(hd + 1)

        def scores(i):
            return jnp.dot(q[i * ATT_Q:(i + 1) * ATT_Q, :], k_t[:, 0:(i + 1) * ATT_Q],
                           preferred_element_type=F32)

        order = list(reversed(range(SEQ // ATT_Q)))
        s_next = scores(order[0])
        for pos, i in enumerate(order):
            q0, n_keys = i * ATT_Q, (i + 1) * ATT_Q
            s, s_next = s_next, (scores(order[pos + 1]) if pos + 1 < len(order) else None)
            s_diag = jnp.where(diag_mask, s[:, q0:n_keys], neg)
            s = s_diag if i == 0 else jnp.concatenate([s[:, :q0], s_diag], axis=1)
            m = jnp.max(s, axis=-1, keepdims=True)
            p = jnp.exp2(s - m).astype(BF16)
            o = jnp.dot(p, v_ones[0:n_keys, :], preferred_element_type=F32)
            o_ref[q0:q0 + ATT_Q, hd * V_DIM:(hd + 1) * V_DIM] = (
                o[:, :V_DIM] / o[:, V_DIM:]).astype(BF16)


def _attention(c_q, c_kv, k_parts, rot_table, wq, wkv):
    def per_batch(width):
        return pl.BlockSpec((None, SEQ, width), lambda b, h: (b, 0, 0))

    def per_head_group(w):
        return pl.BlockSpec((ATT_HEADS,) + w.shape[1:], lambda b, h: (h, 0, 0))

    return pl.pallas_call(
        _attn_kernel,
        grid=(BATCH, MLA_HEADS // ATT_HEADS),
        in_specs=[per_batch(Q_LORA), per_batch(KV_LORA), per_batch(LANES),
                  pl.BlockSpec((2 * ROPE_DIM, SEQ), lambda b, h: (0, b)),
                  per_head_group(wq), per_head_group(wkv)],
        out_specs=pl.BlockSpec((None, SEQ, ATT_HEADS * V_DIM), lambda b, h: (b, 0, h)),
        out_shape=jax.ShapeDtypeStruct((BATCH, SEQ, MLA_HEADS * V_DIM), BF16),
        compiler_params=_params(2),
        name="mla_attention",
    )(c_q, c_kv, k_parts, rot_table, wq, wkv)


def _rope_kernel(inv_freq_ref, pos_ref, out_ref):
    half = ROPE_DIM // 2
    ang = pos_ref[...].astype(F32) * inv_freq_ref[...]
    cos, sin = jnp.cos(ang), jnp.sin(ang)
    out_ref[0:half] = cos
    out_ref[half:2 * half] = cos
    out_ref[2 * half:3 * half] = -sin
    out_ref[3 * half:4 * half] = sin


def _rope_table(positions):
    inv_freq = 1.0 / (ROPE_BASE ** (jnp.arange(0, ROPE_DIM, 2, dtype=F32) / ROPE_DIM))
    return pl.pallas_call(
        _rope_kernel,
        out_shape=jax.ShapeDtypeStruct((2 * ROPE_DIM, positions.size), F32),
        compiler_params=pltpu.CompilerParams(vmem_limit_bytes=VMEM_LIMIT_BYTES),
        name="rope_table",
    )(inv_freq.reshape(ROPE_DIM // 2, 1), positions.reshape(1, positions.size))


def _with_rot_cols(w):
    half = ROPE_DIM // 2
    return jnp.concatenate([w, w[..., half:], w[..., :half]], axis=-1)


def kernel(x, positions, norm_gains, ffn_w_in, ffn_w_out, s5_lambda_re, s5_lambda_im, s5_log_step, s5_b_re, s5_b_im, s5_c_re, s5_c_im, s5_d, s5_w_glu, kv_in_norm, w_dkv, kv_latent_norm, w_ukv, mla_w_dq, mla_q_norm, mla_w_uq, mla_w_o, final_norm):
    assert x.shape == (BATCH, SEQ, D_MODEL) and norm_gains.shape[0] == 2
    rows = BATCH * SEQ
    gains = norm_gains.astype(F32).reshape(2, 3, 1, D_MODEL)
    w_in_f32, w_out_f32 = ffn_w_in.astype(F32), ffn_w_out.astype(F32)

    def ffn_weights(layer, half):
        return ((w_in_f32, (layer, half)), (w_out_f32, (layer, half)))

    xt, w_in, w_out, w_glu = _ffn(
        x.astype(F32), gains[0, 0], w_in_f32[0, 0].astype(BF16), w_out_f32[0, 0].astype(BF16),
        relayout="bt_to_tb", cast_next=ffn_weights(0, 1) + ((s5_w_glu.astype(F32), (0,)),))
    wb, lam_tile, wc = _s5_weights(s5_lambda_re[0], s5_lambda_im[0], s5_log_step[0],
                                   s5_b_re[0], s5_b_im[0], s5_c_re[0], s5_c_im[0])
    xt = _s5(xt, gains[0, 1], wb, lam_tile, wc, s5_d[0].astype(F32).reshape(1, D_MODEL), w_glu)
    wd = jnp.concatenate([w_dkv[:, :KV_LORA], _with_rot_cols(w_dkv[:, KV_LORA:])], axis=-1)
    xb, c_kv, k_parts, w_in, w_out = _ffn(
        xt, gains[0, 2], w_in, w_out, relayout="tb_to_bt", tail="kv",
        tail_args=(kv_in_norm.astype(F32).reshape(1, D_MODEL), wd.astype(BF16),
                   kv_latent_norm.astype(F32).reshape(1, KV_LORA)),
        cast_next=ffn_weights(1, 0))
    xb = xb.reshape(rows, D_MODEL)

    xb, c_q, w_in, w_out, w_o = _ffn(
        xb, gains[1, 0], w_in, w_out, tail="q",
        tail_args=(gains[1, 1], mla_w_dq[0].astype(BF16),
                   mla_q_norm[0].astype(F32).reshape(1, Q_LORA)),
        cast_next=ffn_weights(1, 1) + ((mla_w_o.astype(F32), (0,)),))
    w_uq_h = mla_w_uq[0].reshape(Q_LORA, MLA_HEADS, NOPE_DIM + ROPE_DIM)
    wq = jnp.concatenate([w_uq_h[..., :NOPE_DIM], _with_rot_cols(w_uq_h[..., NOPE_DIM:])], axis=-1)
    wq = wq.transpose(1, 0, 2).astype(BF16)
    wkv = w_ukv.reshape(KV_LORA, MLA_HEADS, NOPE_DIM + V_DIM).transpose(1, 0, 2).astype(BF16)
    o = _attention(c_q.reshape(BATCH, SEQ, Q_LORA), c_kv, k_parts, _rope_table(positions), wq, wkv)
    (out,) = _ffn(xb, gains[1, 2], w_in, w_out,
                  attn=(o.reshape(rows, MLA_HEADS * V_DIM), w_o),
                  tail="final", tail_args=(final_norm.astype(F32).reshape(1, D_MODEL),))
    return out.reshape(BATCH, SEQ, D_MODEL).astype(x.dtype)
```

```python
import functools
import math

import jax
import jax.numpy as jnp
from jax import lax
from jax.experimental import pallas as pl
from jax.experimental.pallas import tpu as pltpu

D_MODEL = 1024
BATCH = 8
SEQ = 2048
CHUNK = 64
EPS = 1e-6
D_FF = 2816
S5_GROUP = 16
S5_GROUPS = D_MODEL // S5_GROUP
S5_STATE = 64
MLA_HEADS = 8
NOPE_DIM = 128
ROPE_DIM = 64
V_DIM = 128
Q_LORA = 256
KV_LORA = 128
ROPE_BASE = 10000.0

LANES = 128
SUBLANES = 8
MXU_DIM = 256
VMEM_LIMIT_BYTES = 56 * 1024 * 1024

FFN_STEPS = 128
FFN_ROWS = FFN_STEPS * BATCH
FFN_SUBTILES = 2
SUB_STEPS = FFN_STEPS // FFN_SUBTILES
SUB_ROWS = FFN_ROWS // FFN_SUBTILES
FFN_CHUNKS = ((0, 1536), (1536, 2816))
S5_STEPS = 64
S5_ROWS = S5_STEPS * BATCH
S5_KT = D_MODEL // MXU_DIM
S5_GPT = MXU_DIM // S5_GROUP
S5_MODES = S5_GPT * S5_STATE
S5_HALF = S5_MODES // 2
ATT_Q = 256
ATT_HEADS = 4

BF16 = jnp.bfloat16
F32 = jnp.float32


def _rms(x, gain):
    return x * lax.rsqrt(jnp.mean(x * x, axis=-1, keepdims=True) + EPS) * gain


def _resident(shape):
    zeros = (0,) * len(shape)
    return pl.BlockSpec(shape, lambda *_: zeros, pipeline_mode=pl.Buffered(1))


def _params(n_grid_dims):
    return pltpu.CompilerParams(
        dimension_semantics=("arbitrary",) * n_grid_dims,
        vmem_limit_bytes=VMEM_LIMIT_BYTES)


_TAIL_INPUTS = {None: 0, "final": 1, "kv": 3, "q": 3}
_TAIL_OUTPUTS = {None: 0, "final": 0, "kv": 2, "q": 1}


def _ffn_kernel(*refs, relayout, with_attn, tail, n_casts):
    refs = list(refs)
    x_ref = refs.pop(0)
    if with_attn:
        o_ref, wo_ref = refs.pop(0), refs.pop(0)
    g_ref, win_ref, wout_ref = refs.pop(0), refs.pop(0), refs.pop(0)
    tail_refs = [refs.pop(0) for _ in range(_TAIL_INPUTS[tail])]
    cast_srcs = [refs.pop(0) for _ in range(n_casts)]
    out_ref = refs.pop(0)
    tail_outs = [refs.pop(0) for _ in range(_TAIL_OUTPUTS[tail])]
    for src_ref, dst_ref in zip(cast_srcs, refs, strict=True):
        dst_ref[...] = src_ref[...].astype(BF16)

    def slices(sub):
        return (slice(sub * SUB_STEPS, (sub + 1) * SUB_STEPS),
                slice(sub * SUB_ROWS, (sub + 1) * SUB_ROWS))

    def head(sub):
        steps, rws = slices(sub)
        if relayout == "bt_to_tb":
            x = jnp.swapaxes(x_ref[:, steps, :], 0, 1).reshape(SUB_ROWS, D_MODEL)
        elif relayout == "tb_to_bt":
            x = jnp.swapaxes(x_ref[rws, :].reshape(SUB_STEPS, BATCH, D_MODEL), 0, 1)
            x = x.reshape(SUB_ROWS, D_MODEL)
        else:
            x = x_ref[rws, :]
        if with_attn:
            x = x + jnp.dot(o_ref[rws, :], wo_ref[...], preferred_element_type=F32)
        return x, _rms(x, g_ref[...]).astype(BF16)

    def chunk(h, acc, c0, c1):
        gate = jnp.dot(h, win_ref[:, c0:c1], preferred_element_type=F32)
        up = jnp.dot(h, win_ref[:, D_FF + c0:D_FF + c1], preferred_element_type=F32)
        a = (jax.nn.silu(gate) * up).astype(BF16)
        part = jnp.dot(a, wout_ref[c0:c1, :], preferred_element_type=F32)
        return part if acc is None else acc + part

    def finish(sub, x, acc):
        steps, rws = slices(sub)
        y = x + 0.5 * acc
        cube = (BATCH, SUB_STEPS, -1)
        if tail == "final":
            (gf_ref,) = tail_refs
            y = _rms(y, gf_ref[...])
        elif tail == "kv":
            gin_ref, wd_ref, glat_ref = tail_refs
            ckv_ref, kr_ref = tail_outs
            kv_a = jnp.dot(_rms(y, gin_ref[...]).astype(BF16), wd_ref[...], preferred_element_type=F32)
            ckv_ref[:, steps, :] = _rms(kv_a[:, :KV_LORA], glat_ref[...]).astype(BF16).reshape(cube)
            kr_ref[:, steps, :] = kv_a[:, KV_LORA:].reshape(cube)
        elif tail == "q":
            gm_ref, wdq_ref, gq_ref = tail_refs
            (cq_ref,) = tail_outs
            c_q = jnp.dot(_rms(y, gm_ref[...]).astype(BF16), wdq_ref[...].astype(BF16),
                          preferred_element_type=F32)
            cq_ref[rws, :] = _rms(c_q, gq_ref[...]).astype(BF16)
        if relayout == "tb_to_bt":
            out_ref[:, steps, :] = y.reshape(cube)
        else:
            out_ref[rws, :] = y

    for sub in range(FFN_SUBTILES):
        x, h = head(sub)
        acc = None
        for c0, c1 in FFN_CHUNKS:
            acc = chunk(h, acc, c0, c1)
        finish(sub, x, acc)


def _ffn(x, gain, w_in, w_out, relayout=None, attn=None, tail=None, tail_args=(), cast_next=()):
    rows = BATCH * SEQ
    n_steps = rows // FFN_ROWS
    row_spec = pl.BlockSpec((FFN_ROWS, D_MODEL), lambda i: (i, 0))
    cube_spec = pl.BlockSpec((BATCH, FFN_STEPS, D_MODEL), lambda i: (0, i, 0))
    flat = jax.ShapeDtypeStruct((rows, D_MODEL), F32)
    cube = jax.ShapeDtypeStruct((BATCH, SEQ, D_MODEL), F32)
    in_spec = cube_spec if relayout == "bt_to_tb" else row_spec
    out_spec, out_shape = (cube_spec, cube) if relayout == "tb_to_bt" else (row_spec, flat)

    args, specs = [x], [in_spec]
    if attn is not None:
        o, w_o = attn
        args += [o, w_o]
        specs += [row_spec, _resident(w_o.shape)]
    args += [gain, w_in, w_out]
    specs += [_resident(gain.shape), _resident(w_in.shape), _resident(w_out.shape)]
    assert len(tail_args) == _TAIL_INPUTS[tail]
    args += list(tail_args)
    specs += [_resident(a.shape) for a in tail_args]
    out_specs, out_shapes = [out_spec], [out_shape]
    if tail == "kv":
        assert relayout == "tb_to_bt"
        out_specs += [pl.BlockSpec((BATCH, FFN_STEPS, LANES), lambda i: (0, i, 0))] * 2
        out_shapes += [jax.ShapeDtypeStruct((BATCH, SEQ, KV_LORA), BF16),
                       jax.ShapeDtypeStruct((BATCH, SEQ, LANES), F32)]
    elif tail == "q":
        out_specs.append(pl.BlockSpec((FFN_ROWS, Q_LORA), lambda i: (i, 0)))
        out_shapes.append(jax.ShapeDtypeStruct((rows, Q_LORA), BF16))
    for src, lead in cast_next:
        n_rows, n_cols = src.shape[len(lead):]
        slab = n_rows // n_steps
        assert slab * n_steps == n_rows and slab % 16 == 0
        args.append(src)
        specs.append(pl.BlockSpec((None,) * len(lead) + (slab, n_cols),
                                  lambda i, lead=lead: lead + (i, 0)))
        out_specs.append(pl.BlockSpec((slab, n_cols), lambda i: (i, 0)))
        out_shapes.append(jax.ShapeDtypeStruct((n_rows, n_cols), BF16))
    return pl.pallas_call(
        functools.partial(_ffn_kernel, relayout=relayout, with_attn=attn is not None, tail=tail,
                          n_casts=len(cast_next)),
        grid=(n_steps,),
        in_specs=specs,
        out_specs=out_specs,
        out_shape=out_shapes,
        compiler_params=_params(1),
        name="ffn",
    )(*args)


def _s5_kernel(x_ref, g_ref, wb_ref, lam_ref, wc_ref, d_ref, wglu_ref, out_ref,
               bu_ref, state_ref):
    @pl.when(pl.program_id(0) == 0)
    def _():
        state_ref[...] = jnp.zeros_like(state_ref)

    x = x_ref[...]
    h = _rms(x, g_ref[...])
    hb = h.astype(BF16)

    def project_in(kt):
        bu_ref[kt] = jnp.dot(hb[:, kt * MXU_DIM:(kt + 1) * MXU_DIM], wb_ref[kt],
                             preferred_element_type=F32)

    def scan(kt):
        for half in range(2):
            re0, im0 = half * S5_HALF, S5_MODES + half * S5_HALF
            lam_re = lam_ref[kt, :, re0:re0 + S5_HALF]
            lam_im = lam_ref[kt, :, im0:im0 + S5_HALF]
            s_re = state_ref[kt, :, re0:re0 + S5_HALF]
            s_im = state_ref[kt, :, im0:im0 + S5_HALF]
            for t in range(S5_STEPS):
                r0 = t * SUBLANES
                n_re = lam_re * s_re - lam_im * s_im + bu_ref[kt, r0:r0 + SUBLANES, re0:re0 + S5_HALF]
                n_im = lam_re * s_im + lam_im * s_re + bu_ref[kt, r0:r0 + SUBLANES, im0:im0 + S5_HALF]
                bu_ref[kt, r0:r0 + SUBLANES, re0:re0 + S5_HALF] = n_re
                bu_ref[kt, r0:r0 + SUBLANES, im0:im0 + S5_HALF] = n_im
                s_re, s_im = n_re, n_im
            state_ref[kt, :, re0:re0 + S5_HALF] = s_re
            state_ref[kt, :, im0:im0 + S5_HALF] = s_im

    ys = []
    project_in(0)
    for kt in range(S5_KT):
        if kt + 1 < S5_KT:
            project_in(kt + 1)
        scan(kt)
        ys.append(jnp.dot(bu_ref[kt].astype(BF16), wc_ref[kt], preferred_element_type=F32))
    y = jnp.concatenate(ys, axis=1) + d_ref[...] * h
    y = jax.nn.gelu(y, approximate=True).astype(BF16)
    z = jnp.dot(y, wglu_ref[...], preferred_element_type=F32)
    out_ref[...] = x + z[:, :D_MODEL] * jax.nn.sigmoid(z[:, D_MODEL:])


def _s5(x, gain, wb, lam, wc, d_skip, w_glu):
    rows = x.shape[0]
    row_spec = pl.BlockSpec((S5_ROWS, D_MODEL), lambda i: (i, 0))
    return pl.pallas_call(
        _s5_kernel,
        grid=(rows // S5_ROWS,),
        in_specs=[row_spec, _resident(gain.shape), _resident(wb.shape), _resident(lam.shape),
                  _resident(wc.shape), _resident(d_skip.shape), _resident(w_glu.shape)],
        out_specs=row_spec,
        out_shape=jax.ShapeDtypeStruct((rows, D_MODEL), F32),
        scratch_shapes=[pltpu.VMEM((S5_KT, S5_ROWS, 2 * S5_MODES), F32),
                        pltpu.VMEM((S5_KT, SUBLANES, 2 * S5_MODES), F32)],
        compiler_params=_params(1),
        name="s5_mixer",
    )(x, gain, wb, lam, wc, d_skip, w_glu)


def _s5_pack_kernel(lam_gc_ref, step_gc_ref, b_ref, c_ref, lam_row_ref, step_row_ref,
                    wb_ref, wc_ref, lam_tile_ref):
    def lam_bar(lam_re, lam_im, log_step):
        step = jnp.exp(log_step)
        mag = jnp.exp(lam_re * step)
        return mag * jnp.cos(lam_im * step), mag * jnp.sin(lam_im * step)

    def spread(x, width):
        k = x.shape[1]
        rep = (lax.broadcasted_iota(jnp.int32, (k, width), 1) % k
               == lax.broadcasted_iota(jnp.int32, (k, width), 0)).astype(F32)
        return jnp.dot(x, rep, preferred_element_type=F32)

    lbar_re, lbar_im = lam_bar(lam_row_ref[0], lam_row_ref[1], step_row_ref[...])
    tile = (S5_KT, SUBLANES, S5_MODES)
    lam_tile_ref[:, :, :S5_MODES] = jnp.broadcast_to(lbar_re[:, None, :], tile)
    lam_tile_ref[:, :, S5_MODES:] = jnp.broadcast_to(lbar_im[:, None, :], tile)

    lam_re, lam_im = lam_gc_ref[0], lam_gc_ref[1]
    lb_re, lb_im = lam_bar(lam_re, lam_im, step_gc_ref[...])
    n_re, n_im = lb_re - 1.0, lb_im
    den = lam_re * lam_re + lam_im * lam_im
    f_re = (n_re * lam_re + n_im * lam_im) / den
    f_im = (n_im * lam_re - n_re * lam_im) / den
    b_re, b_im = b_ref[0], b_ref[1]
    shape = (S5_GROUPS * S5_GROUP, S5_MODES)
    own = (lax.broadcasted_iota(jnp.int32, shape, 0) // S5_GROUP % S5_GPT
           == lax.broadcasted_iota(jnp.int32, shape, 1) // S5_STATE)
    for part, bbar in enumerate((f_re * b_re - f_im * b_im, f_re * b_im + f_im * b_re)):
        wb_ref[:, part * S5_MODES:(part + 1) * S5_MODES] = jnp.where(
            own, spread(bbar, S5_MODES), 0.0).astype(BF16)

    shape = (S5_GROUPS * S5_STATE, MXU_DIM)
    own = (lax.broadcasted_iota(jnp.int32, shape, 0) // S5_STATE % S5_GPT
           == lax.broadcasted_iota(jnp.int32, shape, 1) // S5_GROUP)
    for part, sign in enumerate((1.0, -1.0)):
        packed = jnp.where(own, sign * spread(c_ref[part], MXU_DIM), 0.0).astype(BF16)
        for kt in range(S5_KT):
            wc_ref[kt, part] = packed[kt * S5_MODES:(kt + 1) * S5_MODES]


def _s5_weights(lam_re, lam_im, log_step, b_re, b_im, c_re, c_im):
    lam = jnp.stack([lam_re, lam_im]).astype(F32)
    step = log_step.astype(F32)
    lam_gc = jnp.repeat(lam, S5_GROUP, axis=1)
    step_gc = jnp.repeat(step, S5_GROUP).reshape(S5_GROUPS * S5_GROUP, 1)
    b_gc = jnp.stack([b_re, b_im]).astype(F32).transpose(0, 1, 3, 2)
    b_gc = b_gc.reshape(2, S5_GROUPS * S5_GROUP, S5_STATE)
    c_gp = jnp.stack([c_re, c_im]).astype(F32).transpose(0, 1, 3, 2)
    c_gp = c_gp.reshape(2, S5_GROUPS * S5_STATE, S5_GROUP)
    lam_row = lam.reshape(2, S5_KT, S5_MODES)
    step_row = jnp.repeat(step, S5_STATE).reshape(S5_KT, S5_MODES)
    wb, wc, lam_tile = pl.pallas_call(
        _s5_pack_kernel,
        out_shape=[jax.ShapeDtypeStruct((S5_GROUPS * S5_GROUP, 2 * S5_MODES), BF16),
                   jax.ShapeDtypeStruct((S5_KT, 2, S5_MODES, MXU_DIM), BF16),
                   jax.ShapeDtypeStruct((S5_KT, SUBLANES, 2 * S5_MODES), F32)],
        compiler_params=pltpu.CompilerParams(vmem_limit_bytes=VMEM_LIMIT_BYTES),
        name="s5_pack",
    )(lam_gc, step_gc, b_gc, c_gp, lam_row, step_row)
    return (wb.reshape(S5_KT, MXU_DIM, 2 * S5_MODES), lam_tile,
            wc.reshape(S5_KT, 2 * S5_MODES, MXU_DIM))


def _attn_kernel(cq_ref, ckv_ref, kr_ref, rot_ref, wq_ref, wkv_ref, o_ref):
    scale = math.log2(math.e) / math.sqrt(NOPE_DIM + ROPE_DIM)
    rot = rot_ref[...].T
    low_lanes = lax.broadcasted_iota(jnp.int32, (SEQ, LANES), 1) < ROPE_DIM

    def rope(parts):
        t = parts * rot
        return jnp.where(low_lanes, t + pltpu.roll(t, ROPE_DIM, axis=1), 0.0)

    k_rope_t = rope(kr_ref[...]).T.astype(BF16)
    ones = jnp.ones((SEQ, V_DIM), BF16)
    row_chunk = lax.broadcasted_iota(jnp.int32, (ATT_Q, ATT_Q), 0) // CHUNK
    col_chunk = lax.broadcasted_iota(jnp.int32, (ATT_Q, ATT_Q), 1) // CHUNK
    diag_mask = col_chunk <= row_chunk
    neg = jnp.finfo(F32).min

    def project(hd):
        q = jnp.dot(cq_ref[...], wq_ref[hd], preferred_element_type=F32)
        q = jnp.concatenate([(q[:, :NOPE_DIM] * scale).astype(BF16),
                             (rope(q[:, NOPE_DIM:]) * scale).astype(BF16)], axis=1)
        kv = jnp.dot(ckv_ref[...], wkv_ref[hd], preferred_element_type=F32)
        k_t = jnp.concatenate([kv[:, :NOPE_DIM].T.astype(BF16), k_rope_t], axis=0)
        v_ones = jnp.concatenate([kv[:, NOPE_DIM:].astype(BF16), ones], axis=1)
        return q, k_t, v_ones

    projected = project(0)
    for hd in range(ATT_HEADS):
        q, k_t, v_ones = projected
        if hd + 1 < ATT_HEADS:
            projected = project(hd + 1)

        def scores(i):
            return jnp.dot(q[i * ATT_Q:(i + 1) * ATT_Q, :], k_t[:, 0:(i + 1) * ATT_Q],
                           preferred_element_type=F32)

        order = list(reversed(range(SEQ // ATT_Q)))
        s_next = scores(order[0])
        for pos, i in enumerate(order):
            q0, n_keys = i * ATT_Q, (i + 1) * ATT_Q
            s, s_next = s_next, (scores(order[pos + 1]) if pos + 1 < len(order) else None)
            s_diag = jnp.where(diag_mask, s[:, q0:n_keys], neg)
            s = s_diag if i == 0 else jnp.concatenate([s[:, :q0], s_diag], axis=1)
            m = jnp.max(s, axis=-1, keepdims=True)
            p = jnp.exp2(s - m).astype(BF16)
            o = jnp.dot(p, v_ones[0:n_keys, :], preferred_element_type=F32)
            o_ref[q0:q0 + ATT_Q, hd * V_DIM:(hd + 1) * V_DIM] = (
                o[:, :V_DIM] / o[:, V_DIM:]).astype(BF16)


def _attention(c_q, c_kv, k_parts, rot_table, wq, wkv):
    def per_batch(width):
        return pl.BlockSpec((None, SEQ, width), lambda b, h: (b, 0, 0))

    def per_head_group(w):
        return pl.BlockSpec((ATT_HEADS,) + w.shape[1:], lambda b, h: (h, 0, 0))

    return pl.pallas_call(
        _attn_kernel,
        grid=(BATCH, MLA_HEADS // ATT_HEADS),
        in_specs=[per_batch(Q_LORA), per_batch(KV_LORA), per_batch(LANES),
                  pl.BlockSpec((2 * ROPE_DIM, SEQ), lambda b, h: (0, b)),
                  per_head_group(wq), per_head_group(wkv)],
        out_specs=pl.BlockSpec((None, SEQ, ATT_HEADS * V_DIM), lambda b, h: (b, 0, h)),
        out_shape=jax.ShapeDtypeStruct((BATCH, SEQ, MLA_HEADS * V_DIM), BF16),
        compiler_params=_params(2),
        name="mla_attention",
    )(c_q, c_kv, k_parts, rot_table, wq, wkv)


def _rope_kernel(inv_freq_ref, pos_ref, *refs):
    n_casts = (len(refs) - 1) // 2
    out_ref = refs[n_casts]
    for src_ref, dst_ref in zip(refs[:n_casts], refs[n_casts + 1:], strict=True):
        dst_ref[...] = src_ref[...].astype(BF16)
    half = ROPE_DIM // 2
    ang = pos_ref[...].astype(F32) * inv_freq_ref[...]
    cos, sin = jnp.cos(ang), jnp.sin(ang)
    out_ref[0:half] = cos
    out_ref[half:2 * half] = cos
    out_ref[2 * half:3 * half] = -sin
    out_ref[3 * half:4 * half] = sin


def _rope_table(positions, cast=()):
    n_steps = 16
    tokens = positions.size // n_steps
    inv_freq = 1.0 / (ROPE_BASE ** (jnp.arange(0, ROPE_DIM, 2, dtype=F32) / ROPE_DIM))
    args = [inv_freq.reshape(ROPE_DIM // 2, 1), positions.reshape(1, positions.size)]
    specs = [_resident((ROPE_DIM // 2, 1)), pl.BlockSpec((1, tokens), lambda i: (0, i))]
    out_specs = [pl.BlockSpec((2 * ROPE_DIM, tokens), lambda i: (0, i))]
    out_shapes = [jax.ShapeDtypeStruct((2 * ROPE_DIM, positions.size), F32)]
    for src, lead in cast:
        n_rows, n_cols = src.shape[len(lead):]
        slab = n_rows // n_steps
        assert slab * n_steps == n_rows and slab % 16 == 0
        args.append(src)
        specs.append(pl.BlockSpec((None,) * len(lead) + (slab, n_cols),
                                  lambda i, lead=lead: lead + (i, 0)))
        out_specs.append(pl.BlockSpec((slab, n_cols), lambda i: (i, 0)))
        out_shapes.append(jax.ShapeDtypeStruct((n_rows, n_cols), BF16))
    return pl.pallas_call(
        _rope_kernel,
        grid=(n_steps,),
        in_specs=specs,
        out_specs=out_specs,
        out_shape=out_shapes,
        compiler_params=_params(1),
        name="rope_table",
    )(*args)


def _with_rot_cols(w):
    half = ROPE_DIM // 2
    return jnp.concatenate([w, w[..., half:], w[..., :half]], axis=-1)


def kernel(x, positions, norm_gains, ffn_w_in, ffn_w_out, s5_lambda_re, s5_lambda_im, s5_log_step, s5_b_re, s5_b_im, s5_c_re, s5_c_im, s5_d, s5_w_glu, kv_in_norm, w_dkv, kv_latent_norm, w_ukv, mla_w_dq, mla_q_norm, mla_w_uq, mla_w_o, final_norm):
    assert x.shape == (BATCH, SEQ, D_MODEL) and norm_gains.shape[0] == 2
    rows = BATCH * SEQ
    gains = norm_gains.astype(F32).reshape(2, 3, 1, D_MODEL)
    w_in_f32, w_out_f32 = ffn_w_in.astype(F32), ffn_w_out.astype(F32)

    def ffn_weights(layer, half):
        return ((w_in_f32, (layer, half)), (w_out_f32, (layer, half)))

    rot_table, w_in, w_out = _rope_table(positions, cast=ffn_weights(0, 0))
    xt, w_in, w_out, w_glu = _ffn(
        x.astype(F32), gains[0, 0], w_in, w_out,
        relayout="bt_to_tb", cast_next=ffn_weights(0, 1) + ((s5_w_glu.astype(F32), (0,)),))
    wb, lam_tile, wc = _s5_weights(s5_lambda_re[0], s5_lambda_im[0], s5_log_step[0],
                                   s5_b_re[0], s5_b_im[0], s5_c_re[0], s5_c_im[0])
    xt = _s5(xt, gains[0, 1], wb, lam_tile, wc, s5_d[0].astype(F32).reshape(1, D_MODEL), w_glu)
    wd = jnp.concatenate([w_dkv[:, :KV_LORA], _with_rot_cols(w_dkv[:, KV_LORA:])], axis=-1)
    xb, c_kv, k_parts, w_in, w_out = _ffn(
        xt, gains[0, 2], w_in, w_out, relayout="tb_to_bt", tail="kv",
        tail_args=(kv_in_norm.astype(F32).reshape(1, D_MODEL), wd.astype(BF16),
                   kv_latent_norm.astype(F32).reshape(1, KV_LORA)),
        cast_next=ffn_weights(1, 0))
    xb = xb.reshape(rows, D_MODEL)

    xb, c_q, w_in, w_out, w_o = _ffn(
        xb, gains[1, 0], w_in, w_out, tail="q",
        tail_args=(gains[1, 1], mla_w_dq[0].astype(F32),
                   mla_q_norm[0].astype(F32).reshape(1, Q_LORA)),
        cast_next=ffn_weights(1, 1) + ((mla_w_o.astype(F32), (0,)),))
    w_uq_h = mla_w_uq[0].reshape(Q_LORA, MLA_HEADS, NOPE_DIM + ROPE_DIM)
    wq = jnp.concatenate([w_uq_h[..., :NOPE_DIM], _with_rot_cols(w_uq_h[..., NOPE_DIM:])], axis=-1)
    wq = wq.transpose(1, 0, 2).astype(BF16)
    wkv = w_ukv.reshape(KV_LORA, MLA_HEADS, NOPE_DIM + V_DIM).transpose(1, 0, 2).astype(BF16)
    o = _attention(c_q.reshape(BATCH, SEQ, Q_LORA), c_kv, k_parts, rot_table, wq, wkv)
    (out,) = _ffn(xb, gains[1, 2], w_in, w_out,
                  attn=(o.reshape(rows, MLA_HEADS * V_DIM), w_o),
                  tail="final", tail_args=(final_norm.astype(F32).reshape(1, D_MODEL),))
    return out.reshape(BATCH, SEQ, D_MODEL).astype(x.dtype)
```

```python
import functools
import math

import jax
import jax.numpy as jnp
from jax import lax
from jax.experimental import pallas as pl
from jax.experimental.pallas import tpu as pltpu

D_MODEL = 1024
BATCH = 8
SEQ = 2048
CHUNK = 64
EPS = 1e-6
D_FF = 2816
S5_GROUP = 16
S5_GROUPS = D_MODEL // S5_GROUP
S5_STATE = 64
MLA_HEADS = 8
NOPE_DIM = 128
ROPE_DIM = 64
V_DIM = 128
Q_LORA = 256
KV_LORA = 128
ROPE_BASE = 10000.0

LANES = 128
SUBLANES = 8
MXU_DIM = 256
VMEM_LIMIT_BYTES = 56 * 1024 * 1024

FFN_STEPS = 128
FFN_ROWS = FFN_STEPS * BATCH
FFN_SUBTILES = 2
SUB_STEPS = FFN_STEPS // FFN_SUBTILES
SUB_ROWS = FFN_ROWS // FFN_SUBTILES
FFN_CHUNKS = ((0, 1536), (1536, 2816))
S5_STEPS = 64
S5_ROWS = S5_STEPS * BATCH
S5_KT = D_MODEL // MXU_DIM
S5_GPT = MXU_DIM // S5_GROUP
S5_MODES = S5_GPT * S5_STATE
S5_HALF = S5_MODES // 2
ATT_Q = 256
ATT_HEADS = 4

BF16 = jnp.bfloat16
F32 = jnp.float32


def _rms(x, gain):
    return x * lax.rsqrt(jnp.mean(x * x, axis=-1, keepdims=True) + EPS) * gain


def _resident(shape):
    zeros = (0,) * len(shape)
    return pl.BlockSpec(shape, lambda *_: zeros, pipeline_mode=pl.Buffered(1))


def _params(n_grid_dims):
    return pltpu.CompilerParams(
        dimension_semantics=("arbitrary",) * n_grid_dims,
        vmem_limit_bytes=VMEM_LIMIT_BYTES)


_TAIL_INPUTS = {None: 0, "final": 1, "kv": 3, "q": 3}
_TAIL_OUTPUTS = {None: 0, "final": 0, "kv": 2, "q": 1}


def _ffn_kernel(*refs, relayout, with_attn, tail, n_casts):
    refs = list(refs)
    x_ref = refs.pop(0)
    if with_attn:
        o_ref, wo_ref = refs.pop(0), refs.pop(0)
    g_ref, win_ref, wout_ref = refs.pop(0), refs.pop(0), refs.pop(0)
    tail_refs = [refs.pop(0) for _ in range(_TAIL_INPUTS[tail])]
    cast_srcs = [refs.pop(0) for _ in range(n_casts)]
    out_ref = refs.pop(0)
    tail_outs = [refs.pop(0) for _ in range(_TAIL_OUTPUTS[tail])]
    for src_ref, dst_ref in zip(cast_srcs, refs, strict=True):
        dst_ref[...] = src_ref[...].astype(BF16)

    def slices(sub):
        return (slice(sub * SUB_STEPS, (sub + 1) * SUB_STEPS),
                slice(sub * SUB_ROWS, (sub + 1) * SUB_ROWS))

    def head(sub):
        steps, rws = slices(sub)
        if relayout == "bt_to_tb":
            x = jnp.swapaxes(x_ref[:, steps, :], 0, 1).reshape(SUB_ROWS, D_MODEL)
        elif relayout == "tb_to_bt":
            x = jnp.swapaxes(x_ref[rws, :].reshape(SUB_STEPS, BATCH, D_MODEL), 0, 1)
            x = x.reshape(SUB_ROWS, D_MODEL)
        else:
            x = x_ref[rws, :]
        if with_attn:
            x = x + jnp.dot(o_ref[rws, :], wo_ref[...], preferred_element_type=F32)
        return x, _rms(x, g_ref[...]).astype(BF16)

    def chunk(h, acc, c0, c1):
        gate = jnp.dot(h, win_ref[:, c0:c1], preferred_element_type=F32)
        up = jnp.dot(h, win_ref[:, D_FF + c0:D_FF + c1], preferred_element_type=F32)
        a = (jax.nn.silu(gate) * up).astype(BF16)
        part = jnp.dot(a, wout_ref[c0:c1, :], preferred_element_type=F32)
        return part if acc is None else acc + part

    def finish(sub, x, acc):
        steps, rws = slices(sub)
        y = x + 0.5 * acc
        cube = (BATCH, SUB_STEPS, -1)
        if tail == "final":
            (gf_ref,) = tail_refs
            y = _rms(y, gf_ref[...])
        elif tail == "kv":
            gin_ref, wd_ref, glat_ref = tail_refs
            ckv_ref, kr_ref = tail_outs
            kv_a = jnp.dot(_rms(y, gin_ref[...]).astype(BF16), wd_ref[...], preferred_element_type=F32)
            ckv_ref[:, steps, :] = _rms(kv_a[:, :KV_LORA], glat_ref[...]).astype(BF16).reshape(cube)
            kr_ref[:, steps, :] = kv_a[:, KV_LORA:].reshape(cube)
        elif tail == "q":
            gm_ref, wdq_ref, gq_ref = tail_refs
            (cq_ref,) = tail_outs
            c_q = jnp.dot(_rms(y, gm_ref[...]).astype(BF16), wdq_ref[...], preferred_element_type=F32)
            cq_ref[rws, :] = _rms(c_q, gq_ref[...]).astype(BF16)
        if relayout == "tb_to_bt":
            out_ref[:, steps, :] = y.reshape(cube)
        else:
            out_ref[rws, :] = y

    for sub in range(FFN_SUBTILES):
        x, h = head(sub)
        acc = None
        for c0, c1 in FFN_CHUNKS:
            acc = chunk(h, acc, c0, c1)
        finish(sub, x, acc)


def _ffn(x, gain, w_in, w_out, relayout=None, attn=None, tail=None, tail_args=(), cast_next=()):
    rows = BATCH * SEQ
    n_steps = rows // FFN_ROWS
    row_spec = pl.BlockSpec((FFN_ROWS, D_MODEL), lambda i: (i, 0))
    cube_spec = pl.BlockSpec((BATCH, FFN_STEPS, D_MODEL), lambda i: (0, i, 0))
    flat = jax.ShapeDtypeStruct((rows, D_MODEL), F32)
    cube = jax.ShapeDtypeStruct((BATCH, SEQ, D_MODEL), F32)
    in_spec = cube_spec if relayout == "bt_to_tb" else row_spec
    out_spec, out_shape = (cube_spec, cube) if relayout == "tb_to_bt" else (row_spec, flat)

    args, specs = [x], [in_spec]
    if attn is not None:
        o, w_o = attn
        args += [o, w_o]
        specs += [row_spec, _resident(w_o.shape)]
    args += [gain, w_in, w_out]
    specs += [_resident(gain.shape), _resident(w_in.shape), _resident(w_out.shape)]
    assert len(tail_args) == _TAIL_INPUTS[tail]
    args += list(tail_args)
    specs += [_resident(a.shape) for a in tail_args]
    out_specs, out_shapes = [out_spec], [out_shape]
    if tail == "kv":
        assert relayout == "tb_to_bt"
        out_specs += [pl.BlockSpec((BATCH, FFN_STEPS, LANES), lambda i: (0, i, 0))] * 2
        out_shapes += [jax.ShapeDtypeStruct((BATCH, SEQ, KV_LORA), BF16),
                       jax.ShapeDtypeStruct((BATCH, SEQ, LANES), F32)]
    elif tail == "q":
        out_specs.append(pl.BlockSpec((FFN_ROWS, Q_LORA), lambda i: (i, 0)))
        out_shapes.append(jax.ShapeDtypeStruct((rows, Q_LORA), BF16))
    for src, lead in cast_next:
        n_rows, n_cols = src.shape[len(lead):]
        slab = n_rows // n_steps
        assert slab * n_steps == n_rows and slab % 16 == 0
        args.append(src)
        specs.append(pl.BlockSpec((None,) * len(lead) + (slab, n_cols),
                                  lambda i, lead=lead: lead + (i, 0)))
        out_specs.append(pl.BlockSpec((slab, n_cols), lambda i: (i, 0)))
        out_shapes.append(jax.ShapeDtypeStruct((n_rows, n_cols), BF16))
    return pl.pallas_call(
        functools.partial(_ffn_kernel, relayout=relayout, with_attn=attn is not None, tail=tail,
                          n_casts=len(cast_next)),
        grid=(n_steps,),
        in_specs=specs,
        out_specs=out_specs,
        out_shape=out_shapes,
        compiler_params=_params(1),
        name="ffn",
    )(*args)


def _s5_kernel(x_ref, g_ref, wb_ref, lam_ref, wc_ref, d_ref, wglu_ref, out_ref,
               bu_ref, state_ref):
    @pl.when(pl.program_id(0) == 0)
    def _():
        state_ref[...] = jnp.zeros_like(state_ref)

    x = x_ref[...]
    h = _rms(x, g_ref[...])
    hb = h.astype(BF16)

    def project_in(kt):
        bu_ref[kt] = jnp.dot(hb[:, kt * MXU_DIM:(kt + 1) * MXU_DIM], wb_ref[kt],
                             preferred_element_type=F32)

    def scan(kt):
        for half in range(2):
            re0, im0 = half * S5_HALF, S5_MODES + half * S5_HALF
            lam_re = lam_ref[kt, :, re0:re0 + S5_HALF]
            lam_im = lam_ref[kt, :, im0:im0 + S5_HALF]
            s_re = state_ref[kt, :, re0:re0 + S5_HALF]
            s_im = state_ref[kt, :, im0:im0 + S5_HALF]
            for t in range(S5_STEPS):
                r0 = t * SUBLANES
                n_re = lam_re * s_re - lam_im * s_im + bu_ref[kt, r0:r0 + SUBLANES, re0:re0 + S5_HALF]
                n_im = lam_re * s_im + lam_im * s_re + bu_ref[kt, r0:r0 + SUBLANES, im0:im0 + S5_HALF]
                bu_ref[kt, r0:r0 + SUBLANES, re0:re0 + S5_HALF] = n_re
                bu_ref[kt, r0:r0 + SUBLANES, im0:im0 + S5_HALF] = n_im
                s_re, s_im = n_re, n_im
            state_ref[kt, :, re0:re0 + S5_HALF] = s_re
            state_ref[kt, :, im0:im0 + S5_HALF] = s_im

    ys = []
    project_in(0)
    for kt in range(S5_KT):
        if kt + 1 < S5_KT:
            project_in(kt + 1)
        scan(kt)
        ys.append(jnp.dot(bu_ref[kt].astype(BF16), wc_ref[kt], preferred_element_type=F32))
    y = jnp.concatenate(ys, axis=1) + d_ref[...] * h
    y = jax.nn.gelu(y, approximate=True).astype(BF16)
    gate = jax.nn.sigmoid(jnp.dot(y, wglu_ref[:, D_MODEL:], preferred_element_type=F32))
    out_ref[...] = x + jnp.dot(y, wglu_ref[:, :D_MODEL], preferred_element_type=F32) * gate


def _s5(x, gain, wb, lam, wc, d_skip, w_glu):
    rows = x.shape[0]
    row_spec = pl.BlockSpec((S5_ROWS, D_MODEL), lambda i: (i, 0))
    return pl.pallas_call(
        _s5_kernel,
        grid=(rows // S5_ROWS,),
        in_specs=[row_spec, _resident(gain.shape), _resident(wb.shape), _resident(lam.shape),
                  _resident(wc.shape), _resident(d_skip.shape), _resident(w_glu.shape)],
        out_specs=row_spec,
        out_shape=jax.ShapeDtypeStruct((rows, D_MODEL), F32),
        scratch_shapes=[pltpu.VMEM((S5_KT, S5_ROWS, 2 * S5_MODES), F32),
                        pltpu.VMEM((S5_KT, SUBLANES, 2 * S5_MODES), F32)],
        compiler_params=_params(1),
        name="s5_mixer",
    )(x, gain, wb, lam, wc, d_skip, w_glu)


def _s5_pack_kernel(lam_gc_ref, step_gc_ref, b_ref, c_ref, lam_row_ref, step_row_ref,
                    wb_ref, wc_ref, lam_tile_ref):
    def lam_bar(lam_re, lam_im, log_step):
        step = jnp.exp(log_step)
        mag = jnp.exp(lam_re * step)
        return mag * jnp.cos(lam_im * step), mag * jnp.sin(lam_im * step)

    def spread(x, width):
        k = x.shape[1]
        rep = (lax.broadcasted_iota(jnp.int32, (k, width), 1) % k
               == lax.broadcasted_iota(jnp.int32, (k, width), 0)).astype(F32)
        return jnp.dot(x, rep, preferred_element_type=F32)

    lbar_re, lbar_im = lam_bar(lam_row_ref[0], lam_row_ref[1], step_row_ref[...])
    tile = (S5_KT, SUBLANES, S5_MODES)
    lam_tile_ref[:, :, :S5_MODES] = jnp.broadcast_to(lbar_re[:, None, :], tile)
    lam_tile_ref[:, :, S5_MODES:] = jnp.broadcast_to(lbar_im[:, None, :], tile)

    lam_re, lam_im = lam_gc_ref[0], lam_gc_ref[1]
    lb_re, lb_im = lam_bar(lam_re, lam_im, step_gc_ref[...])
    n_re, n_im = lb_re - 1.0, lb_im
    den = lam_re * lam_re + lam_im * lam_im
    f_re = (n_re * lam_re + n_im * lam_im) / den
    f_im = (n_im * lam_re - n_re * lam_im) / den
    b_re, b_im = b_ref[0], b_ref[1]
    shape = (S5_GROUPS * S5_GROUP, S5_MODES)
    own = (lax.broadcasted_iota(jnp.int32, shape, 0) // S5_GROUP % S5_GPT
           == lax.broadcasted_iota(jnp.int32, shape, 1) // S5_STATE)
    for part, bbar in enumerate((f_re * b_re - f_im * b_im, f_re * b_im + f_im * b_re)):
        wb_ref[:, part * S5_MODES:(part + 1) * S5_MODES] = jnp.where(
            own, spread(bbar, S5_MODES), 0.0).astype(BF16)

    shape = (S5_GROUPS * S5_STATE, MXU_DIM)
    own = (lax.broadcasted_iota(jnp.int32, shape, 0) // S5_STATE % S5_GPT
           == lax.broadcasted_iota(jnp.int32, shape, 1) // S5_GROUP)
    for part, sign in enumerate((1.0, -1.0)):
        packed = jnp.where(own, sign * spread(c_ref[part], MXU_DIM), 0.0).astype(BF16)
        for kt in range(S5_KT):
            wc_ref[kt, part] = packed[kt * S5_MODES:(kt + 1) * S5_MODES]


def _s5_weights(lam_re, lam_im, log_step, b_re, b_im, c_re, c_im):
    lam = jnp.stack([lam_re, lam_im]).astype(F32)
    step = log_step.astype(F32)
    lam_gc = jnp.repeat(lam, S5_GROUP, axis=1)
    step_gc = jnp.repeat(step, S5_GROUP).reshape(S5_GROUPS * S5_GROUP, 1)
    b_gc = jnp.stack([b_re, b_im]).astype(F32).transpose(0, 1, 3, 2)
    b_gc = b_gc.reshape(2, S5_GROUPS * S5_GROUP, S5_STATE)
    c_gp = jnp.stack([c_re, c_im]).astype(F32).transpose(0, 1, 3, 2)
    c_gp = c_gp.reshape(2, S5_GROUPS * S5_STATE, S5_GROUP)
    lam_row = lam.reshape(2, S5_KT, S5_MODES)
    step_row = jnp.repeat(step, S5_STATE).reshape(S5_KT, S5_MODES)
    wb, wc, lam_tile = pl.pallas_call(
        _s5_pack_kernel,
        out_shape=[jax.ShapeDtypeStruct((S5_GROUPS * S5_GROUP, 2 * S5_MODES), BF16),
                   jax.ShapeDtypeStruct((S5_KT, 2, S5_MODES, MXU_DIM), BF16),
                   jax.ShapeDtypeStruct((S5_KT, SUBLANES, 2 * S5_MODES), F32)],
        compiler_params=pltpu.CompilerParams(vmem_limit_bytes=VMEM_LIMIT_BYTES),
        name="s5_pack",
    )(lam_gc, step_gc, b_gc, c_gp, lam_row, step_row)
    return (wb.reshape(S5_KT, MXU_DIM, 2 * S5_MODES), lam_tile,
            wc.reshape(S5_KT, 2 * S5_MODES, MXU_DIM))


def _attn_kernel(cq_ref, ckv_ref, kr_ref, rot_ref, wq_ref, wkv_ref, o_ref):
    scale = math.log2(math.e) / math.sqrt(NOPE_DIM + ROPE_DIM)
    rot = rot_ref[...].T
    low_lanes = lax.broadcasted_iota(jnp.int32, (SEQ, LANES), 1) < ROPE_DIM

    def rope(parts):
        t = parts * rot
        return jnp.where(low_lanes, t + pltpu.roll(t, ROPE_DIM, axis=1), 0.0)

    k_rope_t = rope(kr_ref[...]).T.astype(BF16)
    ones = jnp.ones((SEQ, V_DIM), BF16)
    row_chunk = lax.broadcasted_iota(jnp.int32, (ATT_Q, ATT_Q), 0) // CHUNK
    col_chunk = lax.broadcasted_iota(jnp.int32, (ATT_Q, ATT_Q), 1) // CHUNK
    diag_mask = col_chunk <= row_chunk
    neg = jnp.finfo(F32).min

    def project(hd):
        q = jnp.dot(cq_ref[...], wq_ref[hd], preferred_element_type=F32)
        q = jnp.concatenate([(q[:, :NOPE_DIM] * scale).astype(BF16),
                             (rope(q[:, NOPE_DIM:]) * scale).astype(BF16)], axis=1)
        kv = jnp.dot(ckv_ref[...], wkv_ref[hd], preferred_element_type=F32)
        k_t = jnp.concatenate([kv[:, :NOPE_DIM].T.astype(BF16), k_rope_t], axis=0)
        v_ones = jnp.concatenate([kv[:, NOPE_DIM:].astype(BF16), ones], axis=1)
        return q, k_t, v_ones

    projected = project(0)
    for hd in range(ATT_HEADS):
        q, k_t, v_ones = projected
        if hd + 1 < ATT_HEADS:
            projected = project(hd + 1)

        def scores(i):
            return jnp.dot(q[i * ATT_Q:(i + 1) * ATT_Q, :], k_t[:, 0:(i + 1) * ATT_Q],
                           preferred_element_type=F32)

        order = list(reversed(range(SEQ // ATT_Q)))
        s_next = scores(order[0])
        for pos, i in enumerate(order):
            q0, n_keys = i * ATT_Q, (i + 1) * ATT_Q
            s, s_next = s_next, (scores(order[pos + 1]) if pos + 1 < len(order) else None)
            s_diag = jnp.where(diag_mask, s[:, q0:n_keys], neg)
            s = s_diag if i == 0 else jnp.concatenate([s[:, :q0], s_diag], axis=1)
            m = jnp.max(s, axis=-1, keepdims=True)
            p = jnp.exp2(s - m).astype(BF16)
            o = jnp.dot(p, v_ones[0:n_keys, :], preferred_element_type=F32)
            o_ref[q0:q0 + ATT_Q, hd * V_DIM:(hd + 1) * V_DIM] = (
                o[:, :V_DIM] / o[:, V_DIM:]).astype(BF16)


def _attention(c_q, c_kv, k_parts, rot_table, wq, wkv):
    def per_batch(width):
        return pl.BlockSpec((None, SEQ, width), lambda b, h: (b, 0, 0))

    def per_head_group(w):
        return pl.BlockSpec((ATT_HEADS,) + w.shape[1:], lambda b, h: (h, 0, 0))

    return pl.pallas_call(
        _attn_kernel,
        grid=(BATCH, MLA_HEADS // ATT_HEADS),
        in_specs=[per_batch(Q_LORA), per_batch(KV_LORA), per_batch(LANES),
                  pl.BlockSpec((2 * ROPE_DIM, SEQ), lambda b, h: (0, b)),
                  per_head_group(wq), per_head_group(wkv)],
        out_specs=pl.BlockSpec((None, SEQ, ATT_HEADS * V_DIM), lambda b, h: (b, 0, h)),
        out_shape=jax.ShapeDtypeStruct((BATCH, SEQ, MLA_HEADS * V_DIM), BF16),
        compiler_params=_params(2),
        name="mla_attention",
    )(c_q, c_kv, k_parts, rot_table, wq, wkv)


def _rope_kernel(inv_freq_ref, pos_ref, *refs):
    n_casts = (len(refs) - 1) // 2
    out_ref = refs[n_casts]
    for src_ref, dst_ref in zip(refs[:n_casts], refs[n_casts + 1:], strict=True):
        dst_ref[...] = src_ref[...].astype(BF16)
    half = ROPE_DIM // 2
    ang = pos_ref[...].astype(F32) * inv_freq_ref[...]
    cos, sin = jnp.cos(ang), jnp.sin(ang)
    out_ref[0:half] = cos
    out_ref[half:2 * half] = cos
    out_ref[2 * half:3 * half] = -sin
    out_ref[3 * half:4 * half] = sin


def _rope_table(positions, cast=()):
    n_steps = 16
    tokens = positions.size // n_steps
    inv_freq = 1.0 / (ROPE_BASE ** (jnp.arange(0, ROPE_DIM, 2, dtype=F32) / ROPE_DIM))
    args = [inv_freq.reshape(ROPE_DIM // 2, 1), positions.reshape(1, positions.size)]
    specs = [_resident((ROPE_DIM // 2, 1)), pl.BlockSpec((1, tokens), lambda i: (0, i))]
    out_specs = [pl.BlockSpec((2 * ROPE_DIM, tokens), lambda i: (0, i))]
    out_shapes = [jax.ShapeDtypeStruct((2 * ROPE_DIM, positions.size), F32)]
    for src, lead in cast:
        n_rows, n_cols = src.shape[len(lead):]
        slab = n_rows // n_steps
        assert slab * n_steps == n_rows and slab % 16 == 0
        args.append(src)
        specs.append(pl.BlockSpec((None,) * len(lead) + (slab, n_cols),
                                  lambda i, lead=lead: lead + (i, 0)))
        out_specs.append(pl.BlockSpec((slab, n_cols), lambda i: (i, 0)))
        out_shapes.append(jax.ShapeDtypeStruct((n_rows, n_cols), BF16))
    return pl.pallas_call(
        _rope_kernel,
        grid=(n_steps,),
        in_specs=specs,
        out_specs=out_specs,
        out_shape=out_shapes,
        compiler_params=_params(1),
        name="rope_table",
    )(*args)


def _with_rot_cols(w):
    half = ROPE_DIM // 2
    return jnp.concatenate([w, w[..., half:], w[..., :half]], axis=-1)


def kernel(x, positions, norm_gains, ffn_w_in, ffn_w_out, s5_lambda_re, s5_lambda_im, s5_log_step, s5_b_re, s5_b_im, s5_c_re, s5_c_im, s5_d, s5_w_glu, kv_in_norm, w_dkv, kv_latent_norm, w_ukv, mla_w_dq, mla_q_norm, mla_w_uq, mla_w_o, final_norm):
    assert x.shape == (BATCH, SEQ, D_MODEL) and norm_gains.shape[0] == 2
    rows = BATCH * SEQ
    gains = norm_gains.astype(F32).reshape(2, 3, 1, D_MODEL)
    w_in_f32, w_out_f32 = ffn_w_in.astype(F32), ffn_w_out.astype(F32)

    def ffn_weights(layer, half):
        return ((w_in_f32, (layer, half)), (w_out_f32, (layer, half)))

    rot_table, w_in, w_out = _rope_table(positions, cast=ffn_weights(0, 0))
    xt, w_in, w_out, w_glu = _ffn(
        x.astype(F32), gains[0, 0], w_in, w_out,
        relayout="bt_to_tb", cast_next=ffn_weights(0, 1) + ((s5_w_glu.astype(F32), (0,)),))
    wb, lam_tile, wc = _s5_weights(s5_lambda_re[0], s5_lambda_im[0], s5_log_step[0],
                                   s5_b_re[0], s5_b_im[0], s5_c_re[0], s5_c_im[0])
    xt = _s5(xt, gains[0, 1], wb, lam_tile, wc, s5_d[0].astype(F32).reshape(1, D_MODEL), w_glu)
    wd = jnp.concatenate([w_dkv[:, :KV_LORA], _with_rot_cols(w_dkv[:, KV_LORA:])], axis=-1)
    xb, c_kv, k_parts, w_in, w_out = _ffn(
        xt, gains[0, 2], w_in, w_out, relayout="tb_to_bt", tail="kv",
        tail_args=(kv_in_norm.astype(F32).reshape(1, D_MODEL), wd.astype(BF16),
                   kv_latent_norm.astype(F32).reshape(1, KV_LORA)),
        cast_next=ffn_weights(1, 0))
    xb = xb.reshape(rows, D_MODEL)

    xb, c_q, w_in, w_out, w_o = _ffn(
        xb, gains[1, 0], w_in, w_out, tail="q",
        tail_args=(gains[1, 1], mla_w_dq[0].astype(BF16),
                   mla_q_norm[0].astype(F32).reshape(1, Q_LORA)),
        cast_next=ffn_weights(1, 1) + ((mla_w_o.astype(F32), (0,)),))
    w_uq_h = mla_w_uq[0].reshape(Q_LORA, MLA_HEADS, NOPE_DIM + ROPE_DIM)
    wq = jnp.concatenate([w_uq_h[..., :NOPE_DIM], _with_rot_cols(w_uq_h[..., NOPE_DIM:])], axis=-1)
    wq = wq.transpose(1, 0, 2).astype(BF16)
    wkv = w_ukv.reshape(KV_LORA, MLA_HEADS, NOPE_DIM + V_DIM).transpose(1, 0, 2).astype(BF16)
    o = _attention(c_q.reshape(BATCH, SEQ, Q_LORA), c_kv, k_parts, rot_table, wq, wkv)
    (out,) = _ffn(xb, gains[1, 2], w_in, w_out,
                  attn=(o.reshape(rows, MLA_HEADS * V_DIM), w_o),
                  tail="final", tail_args=(final_norm.astype(F32).reshape(1, D_MODEL),))
    return out.reshape(BATCH, SEQ, D_MODEL).astype(x.dtype)
```

```python
import functools
import math

import jax
import jax.numpy as jnp
from jax import lax
from jax.experimental import pallas as pl
from jax.experimental.pallas import tpu as pltpu

D_MODEL = 1024
BATCH = 8
SEQ = 2048
CHUNK = 64
EPS = 1e-6
D_FF = 2816
S5_GROUP = 16
S5_GROUPS = D_MODEL // S5_GROUP
S5_STATE = 64
MLA_HEADS = 8
NOPE_DIM = 128
ROPE_DIM = 64
V_DIM = 128
Q_LORA = 256
KV_LORA = 128
ROPE_BASE = 10000.0

LANES = 128
SUBLANES = 8
MXU_DIM = 256
VMEM_LIMIT_BYTES = 56 * 1024 * 1024

FFN_STEPS = 128
FFN_ROWS = FFN_STEPS * BATCH
FFN_SUBTILES = 2
SUB_STEPS = FFN_STEPS // FFN_SUBTILES
SUB_ROWS = FFN_ROWS // FFN_SUBTILES
FFN_CHUNKS = ((0, 1536), (1536, 2816))
S5_STEPS = 64
S5_ROWS = S5_STEPS * BATCH
S5_KT = D_MODEL // MXU_DIM
S5_GPT = MXU_DIM // S5_GROUP
S5_MODES = S5_GPT * S5_STATE
S5_HALF = S5_MODES // 2
ATT_Q = 256
ATT_HEADS = 4

BF16 = jnp.bfloat16
F32 = jnp.float32


def _rms(x, gain):
    return x * lax.rsqrt(jnp.mean(x * x, axis=-1, keepdims=True) + EPS) * gain


def _resident(shape):
    zeros = (0,) * len(shape)
    return pl.BlockSpec(shape, lambda *_: zeros, pipeline_mode=pl.Buffered(1))


def _params(n_grid_dims):
    return pltpu.CompilerParams(
        dimension_semantics=("arbitrary",) * n_grid_dims,
        vmem_limit_bytes=VMEM_LIMIT_BYTES)


_TAIL_INPUTS = {None: 0, "final": 1, "kv": 3, "q": 3}
_TAIL_OUTPUTS = {None: 0, "final": 0, "kv": 2, "q": 1}


def _ffn_kernel(*refs, relayout, with_attn, tail, n_casts):
    refs = list(refs)
    x_ref = refs.pop(0)
    if with_attn:
        o_ref, wo_ref = refs.pop(0), refs.pop(0)
    g_ref, win_ref, wout_ref = refs.pop(0), refs.pop(0), refs.pop(0)
    tail_refs = [refs.pop(0) for _ in range(_TAIL_INPUTS[tail])]
    cast_srcs = [refs.pop(0) for _ in range(n_casts)]
    out_ref = refs.pop(0)
    tail_outs = [refs.pop(0) for _ in range(_TAIL_OUTPUTS[tail])]
    for src_ref, dst_ref in zip(cast_srcs, refs, strict=True):
        dst_ref[...] = src_ref[...].astype(BF16)

    def slices(sub):
        return (slice(sub * SUB_STEPS, (sub + 1) * SUB_STEPS),
                slice(sub * SUB_ROWS, (sub + 1) * SUB_ROWS))

    def head(sub):
        steps, rws = slices(sub)
        if relayout == "bt_to_tb":
            x = jnp.swapaxes(x_ref[:, steps, :], 0, 1).reshape(SUB_ROWS, D_MODEL)
        elif relayout == "tb_to_bt":
            x = jnp.swapaxes(x_ref[rws, :].reshape(SUB_STEPS, BATCH, D_MODEL), 0, 1)
            x = x.reshape(SUB_ROWS, D_MODEL)
        else:
            x = x_ref[rws, :]
        if with_attn:
            x = x + jnp.dot(o_ref[rws, :], wo_ref[...], preferred_element_type=F32)
        return x, _rms(x, g_ref[...]).astype(BF16)

    def chunk(h, acc, c0, c1):
        gate = jnp.dot(h, win_ref[:, c0:c1], preferred_element_type=F32)
        up = jnp.dot(h, win_ref[:, D_FF + c0:D_FF + c1], preferred_element_type=F32)
        a = (jax.nn.silu(gate) * up).astype(BF16)
        part = jnp.dot(a, wout_ref[c0:c1, :], preferred_element_type=F32)
        return acc + 0.5 * part

    def finish(sub, y):
        steps, rws = slices(sub)
        cube = (BATCH, SUB_STEPS, -1)
        if tail == "final":
            (gf_ref,) = tail_refs
            y = _rms(y, gf_ref[...])
        elif tail == "kv":
            gin_ref, wd_ref, glat_ref = tail_refs
            ckv_ref, kr_ref = tail_outs
            kv_a = jnp.dot(_rms(y, gin_ref[...]).astype(BF16), wd_ref[...], preferred_element_type=F32)
            ckv_ref[:, steps, :] = _rms(kv_a[:, :KV_LORA], glat_ref[...]).astype(BF16).reshape(cube)
            kr_ref[:, steps, :] = kv_a[:, KV_LORA:].reshape(cube)
        elif tail == "q":
            gm_ref, wdq_ref, gq_ref = tail_refs
            (cq_ref,) = tail_outs
            c_q = jnp.dot(_rms(y, gm_ref[...]).astype(BF16), wdq_ref[...], preferred_element_type=F32)
            cq_ref[rws, :] = _rms(c_q, gq_ref[...]).astype(BF16)
        if relayout == "tb_to_bt":
            out_ref[:, steps, :] = y.reshape(cube)
        else:
            out_ref[rws, :] = y

    for sub in range(FFN_SUBTILES):
        y, h = head(sub)
        for c0, c1 in FFN_CHUNKS:
            y = chunk(h, y, c0, c1)
        finish(sub, y)


def _ffn(x, gain, w_in, w_out, relayout=None, attn=None, tail=None, tail_args=(), cast_next=()):
    rows = BATCH * SEQ
    n_steps = rows // FFN_ROWS
    row_spec = pl.BlockSpec((FFN_ROWS, D_MODEL), lambda i: (i, 0))
    cube_spec = pl.BlockSpec((BATCH, FFN_STEPS, D_MODEL), lambda i: (0, i, 0))
    flat = jax.ShapeDtypeStruct((rows, D_MODEL), F32)
    cube = jax.ShapeDtypeStruct((BATCH, SEQ, D_MODEL), F32)
    in_spec = cube_spec if relayout == "bt_to_tb" else row_spec
    out_spec, out_shape = (cube_spec, cube) if relayout == "tb_to_bt" else (row_spec, flat)

    args, specs = [x], [in_spec]
    if attn is not None:
        o, w_o = attn
        args += [o, w_o]
        specs += [row_spec, _resident(w_o.shape)]
    args += [gain, w_in, w_out]
    specs += [_resident(gain.shape), _resident(w_in.shape), _resident(w_out.shape)]
    assert len(tail_args) == _TAIL_INPUTS[tail]
    args += list(tail_args)
    specs += [_resident(a.shape) for a in tail_args]
    out_specs, out_shapes = [out_spec], [out_shape]
    if tail == "kv":
        assert relayout == "tb_to_bt"
        out_specs += [pl.BlockSpec((BATCH, FFN_STEPS, LANES), lambda i: (0, i, 0))] * 2
        out_shapes += [jax.ShapeDtypeStruct((BATCH, SEQ, KV_LORA), BF16),
                       jax.ShapeDtypeStruct((BATCH, SEQ, LANES), F32)]
    elif tail == "q":
        out_specs.append(pl.BlockSpec((FFN_ROWS, Q_LORA), lambda i: (i, 0)))
        out_shapes.append(jax.ShapeDtypeStruct((rows, Q_LORA), BF16))
    for src, lead in cast_next:
        n_rows, n_cols = src.shape[len(lead):]
        slab = n_rows // n_steps
        assert slab * n_steps == n_rows and slab % 16 == 0
        args.append(src)
        specs.append(pl.BlockSpec((None,) * len(lead) + (slab, n_cols),
                                  lambda i, lead=lead: lead + (i, 0)))
        out_specs.append(pl.BlockSpec((slab, n_cols), lambda i: (i, 0)))
        out_shapes.append(jax.ShapeDtypeStruct((n_rows, n_cols), BF16))
    return pl.pallas_call(
        functools.partial(_ffn_kernel, relayout=relayout, with_attn=attn is not None, tail=tail,
                          n_casts=len(cast_next)),
        grid=(n_steps,),
        in_specs=specs,
        out_specs=out_specs,
        out_shape=out_shapes,
        compiler_params=_params(1),
        name="ffn",
    )(*args)


def _s5_kernel(x_ref, g_ref, wb_ref, lam_ref, wc_ref, d_ref, wglu_ref, out_ref,
               bu_ref, state_ref):
    @pl.when(pl.program_id(0) == 0)
    def _():
        state_ref[...] = jnp.zeros_like(state_ref)

    x = x_ref[...]
    h = _rms(x, g_ref[...])
    hb = h.astype(BF16)

    def project_in(kt):
        bu_ref[kt] = jnp.dot(hb[:, kt * MXU_DIM:(kt + 1) * MXU_DIM], wb_ref[kt],
                             preferred_element_type=F32)

    def scan(kt):
        for half in range(2):
            re0, im0 = half * S5_HALF, S5_MODES + half * S5_HALF
            lam_re = lam_ref[kt, :, re0:re0 + S5_HALF]
            lam_im = lam_ref[kt, :, im0:im0 + S5_HALF]
            s_re = state_ref[kt, :, re0:re0 + S5_HALF]
            s_im = state_ref[kt, :, im0:im0 + S5_HALF]
            for t in range(S5_STEPS):
                r0 = t * SUBLANES
                n_re = lam_re * s_re - lam_im * s_im + bu_ref[kt, r0:r0 + SUBLANES, re0:re0 + S5_HALF]
                n_im = lam_re * s_im + lam_im * s_re + bu_ref[kt, r0:r0 + SUBLANES, im0:im0 + S5_HALF]
                bu_ref[kt, r0:r0 + SUBLANES, re0:re0 + S5_HALF] = n_re
                bu_ref[kt, r0:r0 + SUBLANES, im0:im0 + S5_HALF] = n_im
                s_re, s_im = n_re, n_im
            state_ref[kt, :, re0:re0 + S5_HALF] = s_re
            state_ref[kt, :, im0:im0 + S5_HALF] = s_im

    ys = []
    project_in(0)
    for kt in range(S5_KT):
        if kt + 1 < S5_KT:
            project_in(kt + 1)
        scan(kt)
        ys.append(jnp.dot(bu_ref[kt].astype(BF16), wc_ref[kt], preferred_element_type=F32))
    y = jnp.concatenate(ys, axis=1) + d_ref[...] * h
    y = jax.nn.gelu(y, approximate=True).astype(BF16)
    gate = jax.nn.sigmoid(jnp.dot(y, wglu_ref[:, D_MODEL:], preferred_element_type=F32))
    out_ref[...] = x + jnp.dot(y, wglu_ref[:, :D_MODEL], preferred_element_type=F32) * gate


def _s5(x, gain, wb, lam, wc, d_skip, w_glu):
    rows = x.shape[0]
    row_spec = pl.BlockSpec((S5_ROWS, D_MODEL), lambda i: (i, 0))
    return pl.pallas_call(
        _s5_kernel,
        grid=(rows // S5_ROWS,),
        in_specs=[row_spec, _resident(gain.shape), _resident(wb.shape), _resident(lam.shape),
                  _resident(wc.shape), _resident(d_skip.shape), _resident(w_glu.shape)],
        out_specs=row_spec,
        out_shape=jax.ShapeDtypeStruct((rows, D_MODEL), F32),
        scratch_shapes=[pltpu.VMEM((S5_KT, S5_ROWS, 2 * S5_MODES), F32),
                        pltpu.VMEM((S5_KT, SUBLANES, 2 * S5_MODES), F32)],
        compiler_params=_params(1),
        name="s5_mixer",
    )(x, gain, wb, lam, wc, d_skip, w_glu)


def _s5_pack_kernel(lam_gc_ref, step_gc_ref, b_ref, c_ref, lam_row_ref, step_row_ref,
                    wb_ref, wc_ref, lam_tile_ref):
    def lam_bar(lam_re, lam_im, log_step):
        step = jnp.exp(log_step)
        mag = jnp.exp(lam_re * step)
        return mag * jnp.cos(lam_im * step), mag * jnp.sin(lam_im * step)

    def spread(x, width):
        k = x.shape[1]
        rep = (lax.broadcasted_iota(jnp.int32, (k, width), 1) % k
               == lax.broadcasted_iota(jnp.int32, (k, width), 0)).astype(F32)
        return jnp.dot(x, rep, preferred_element_type=F32)

    lbar_re, lbar_im = lam_bar(lam_row_ref[0], lam_row_ref[1], step_row_ref[...])
    tile = (S5_KT, SUBLANES, S5_MODES)
    lam_tile_ref[:, :, :S5_MODES] = jnp.broadcast_to(lbar_re[:, None, :], tile)
    lam_tile_ref[:, :, S5_MODES:] = jnp.broadcast_to(lbar_im[:, None, :], tile)

    lam_re, lam_im = lam_gc_ref[0], lam_gc_ref[1]
    lb_re, lb_im = lam_bar(lam_re, lam_im, step_gc_ref[...])
    n_re, n_im = lb_re - 1.0, lb_im
    den = lam_re * lam_re + lam_im * lam_im
    f_re = (n_re * lam_re + n_im * lam_im) / den
    f_im = (n_im * lam_re - n_re * lam_im) / den
    b_re, b_im = b_ref[0], b_ref[1]
    shape = (S5_GROUPS * S5_GROUP, S5_MODES)
    own = (lax.broadcasted_iota(jnp.int32, shape, 0) // S5_GROUP % S5_GPT
           == lax.broadcasted_iota(jnp.int32, shape, 1) // S5_STATE)
    for part, bbar in enumerate((f_re * b_re - f_im * b_im, f_re * b_im + f_im * b_re)):
        wb_ref[:, part * S5_MODES:(part + 1) * S5_MODES] = jnp.where(
            own, spread(bbar, S5_MODES), 0.0).astype(BF16)

    shape = (S5_GROUPS * S5_STATE, MXU_DIM)
    own = (lax.broadcasted_iota(jnp.int32, shape, 0) // S5_STATE % S5_GPT
           == lax.broadcasted_iota(jnp.int32, shape, 1) // S5_GROUP)
    for part, sign in enumerate((1.0, -1.0)):
        packed = jnp.where(own, sign * spread(c_ref[part], MXU_DIM), 0.0).astype(BF16)
        for kt in range(S5_KT):
            wc_ref[kt, part] = packed[kt * S5_MODES:(kt + 1) * S5_MODES]


def _s5_weights(lam_re, lam_im, log_step, b_re, b_im, c_re, c_im):
    lam = jnp.stack([lam_re, lam_im]).astype(F32)
    step = log_step.astype(F32)
    lam_gc = jnp.repeat(lam, S5_GROUP, axis=1)
    step_gc = jnp.repeat(step, S5_GROUP).reshape(S5_GROUPS * S5_GROUP, 1)
    b_gc = jnp.stack([b_re, b_im]).astype(F32).transpose(0, 1, 3, 2)
    b_gc = b_gc.reshape(2, S5_GROUPS * S5_GROUP, S5_STATE)
    c_gp = jnp.stack([c_re, c_im]).astype(F32).transpose(0, 1, 3, 2)
    c_gp = c_gp.reshape(2, S5_GROUPS * S5_STATE, S5_GROUP)
    lam_row = lam.reshape(2, S5_KT, S5_MODES)
    step_row = jnp.repeat(step, S5_STATE).reshape(S5_KT, S5_MODES)
    wb, wc, lam_tile = pl.pallas_call(
        _s5_pack_kernel,
        out_shape=[jax.ShapeDtypeStruct((S5_GROUPS * S5_GROUP, 2 * S5_MODES), BF16),
                   jax.ShapeDtypeStruct((S5_KT, 2, S5_MODES, MXU_DIM), BF16),
                   jax.ShapeDtypeStruct((S5_KT, SUBLANES, 2 * S5_MODES), F32)],
        compiler_params=pltpu.CompilerParams(vmem_limit_bytes=VMEM_LIMIT_BYTES),
        name="s5_pack",
    )(lam_gc, step_gc, b_gc, c_gp, lam_row, step_row)
    return (wb.reshape(S5_KT, MXU_DIM, 2 * S5_MODES), lam_tile,
            wc.reshape(S5_KT, 2 * S5_MODES, MXU_DIM))


def _attn_kernel(cq_ref, ckv_ref, kr_ref, rot_ref, wq_ref, wkv_ref, o_ref):
    scale = math.log2(math.e) / math.sqrt(NOPE_DIM + ROPE_DIM)
    rot = rot_ref[...].T
    low_lanes = lax.broadcasted_iota(jnp.int32, (SEQ, LANES), 1) < ROPE_DIM

    def rope(parts):
        t = parts * rot
        return jnp.where(low_lanes, t + pltpu.roll(t, ROPE_DIM, axis=1), 0.0)

    k_rope_t = rope(kr_ref[...]).T.astype(BF16)
    ones = jnp.ones((SEQ, V_DIM), BF16)
    row_chunk = lax.broadcasted_iota(jnp.int32, (ATT_Q, ATT_Q), 0) // CHUNK
    col_chunk = lax.broadcasted_iota(jnp.int32, (ATT_Q, ATT_Q), 1) // CHUNK
    diag_mask = col_chunk <= row_chunk
    neg = jnp.finfo(F32).min

    def project(hd):
        q = jnp.dot(cq_ref[...], wq_ref[hd], preferred_element_type=F32)
        q = jnp.concatenate([(q[:, :NOPE_DIM] * scale).astype(BF16),
                             (rope(q[:, NOPE_DIM:]) * scale).astype(BF16)], axis=1)
        kv = jnp.dot(ckv_ref[...], wkv_ref[hd], preferred_element_type=F32)
        k_t = jnp.concatenate([kv[:, :NOPE_DIM].T.astype(BF16), k_rope_t], axis=0)
        v_ones = jnp.concatenate([kv[:, NOPE_DIM:].astype(BF16), ones], axis=1)
        return q, k_t, v_ones

    projected = project(0)
    for hd in range(ATT_HEADS):
        q, k_t, v_ones = projected
        if hd + 1 < ATT_HEADS:
            projected = project(hd + 1)

        def scores(i):
            return jnp.dot(q[i * ATT_Q:(i + 1) * ATT_Q, :], k_t[:, 0:(i + 1) * ATT_Q],
                           preferred_element_type=F32)

        order = list(reversed(range(SEQ // ATT_Q)))
        s_next = scores(order[0])
        for pos, i in enumerate(order):
            q0, n_keys = i * ATT_Q, (i + 1) * ATT_Q
            s, s_next = s_next, (scores(order[pos + 1]) if pos + 1 < len(order) else None)
            s_diag = jnp.where(diag_mask, s[:, q0:n_keys], neg)
            s = s_diag if i == 0 else jnp.concatenate([s[:, :q0], s_diag], axis=1)
            m = jnp.max(s, axis=-1, keepdims=True)
            p = jnp.exp2(s - m).astype(BF16)
            o = jnp.dot(p, v_ones[0:n_keys, :], preferred_element_type=F32)
            o_ref[q0:q0 + ATT_Q, hd * V_DIM:(hd + 1) * V_DIM] = (
                o[:, :V_DIM] / o[:, V_DIM:]).astype(BF16)


def _attention(c_q, c_kv, k_parts, rot_table, wq, wkv):
    def per_batch(width):
        return pl.BlockSpec((None, SEQ, width), lambda b, h: (b, 0, 0))

    def per_head_group(w):
        return pl.BlockSpec((ATT_HEADS,) + w.shape[1:], lambda b, h: (h, 0, 0))

    return pl.pallas_call(
        _attn_kernel,
        grid=(BATCH, MLA_HEADS // ATT_HEADS),
        in_specs=[per_batch(Q_LORA), per_batch(KV_LORA), per_batch(LANES),
                  pl.BlockSpec((2 * ROPE_DIM, SEQ), lambda b, h: (0, b)),
                  per_head_group(wq), per_head_group(wkv)],
        out_specs=pl.BlockSpec((None, SEQ, ATT_HEADS * V_DIM), lambda b, h: (b, 0, h)),
        out_shape=jax.ShapeDtypeStruct((BATCH, SEQ, MLA_HEADS * V_DIM), BF16),
        compiler_params=_params(2),
        name="mla_attention",
    )(c_q, c_kv, k_parts, rot_table, wq, wkv)


def _rope_kernel(inv_freq_ref, pos_ref, *refs):
    n_casts = (len(refs) - 1) // 2
    out_ref = refs[n_casts]
    for src_ref, dst_ref in zip(refs[:n_casts], refs[n_casts + 1:], strict=True):
        dst_ref[...] = src_ref[...].astype(BF16)
    half = ROPE_DIM // 2
    ang = pos_ref[...].astype(F32) * inv_freq_ref[...]
    cos, sin = jnp.cos(ang), jnp.sin(ang)
    out_ref[0:half] = cos
    out_ref[half:2 * half] = cos
    out_ref[2 * half:3 * half] = -sin
    out_ref[3 * half:4 * half] = sin


def _rope_table(positions, cast=()):
    n_steps = 16
    tokens = positions.size // n_steps
    inv_freq = 1.0 / (ROPE_BASE ** (jnp.arange(0, ROPE_DIM, 2, dtype=F32) / ROPE_DIM))
    args = [inv_freq.reshape(ROPE_DIM // 2, 1), positions.reshape(1, positions.size)]
    specs = [_resident((ROPE_DIM // 2, 1)), pl.BlockSpec((1, tokens), lambda i: (0, i))]
    out_specs = [pl.BlockSpec((2 * ROPE_DIM, tokens), lambda i: (0, i))]
    out_shapes = [jax.ShapeDtypeStruct((2 * ROPE_DIM, positions.size), F32)]
    for src, lead in cast:
        n_rows, n_cols = src.shape[len(lead):]
        slab = n_rows // n_steps
        assert slab * n_steps == n_rows and slab % 16 == 0
        args.append(src)
        specs.append(pl.BlockSpec((None,) * len(lead) + (slab, n_cols),
                                  lambda i, lead=lead: lead + (i, 0)))
        out_specs.append(pl.BlockSpec((slab, n_cols), lambda i: (i, 0)))
        out_shapes.append(jax.ShapeDtypeStruct((n_rows, n_cols), BF16))
    return pl.pallas_call(
        _rope_kernel,
        grid=(n_steps,),
        in_specs=specs,
        out_specs=out_specs,
        out_shape=out_shapes,
        compiler_params=_params(1),
        name="rope_table",
    )(*args)


def _with_rot_cols(w):
    half = ROPE_DIM // 2
    return jnp.concatenate([w, w[..., half:], w[..., :half]], axis=-1)


def kernel(x, positions, norm_gains, ffn_w_in, ffn_w_out, s5_lambda_re, s5_lambda_im, s5_log_step, s5_b_re, s5_b_im, s5_c_re, s5_c_im, s5_d, s5_w_glu, kv_in_norm, w_dkv, kv_latent_norm, w_ukv, mla_w_dq, mla_q_norm, mla_w_uq, mla_w_o, final_norm):
    assert x.shape == (BATCH, SEQ, D_MODEL) and norm_gains.shape[0] == 2
    rows = BATCH * SEQ
    gains = norm_gains.astype(F32).reshape(2, 3, 1, D_MODEL)
    w_in_f32, w_out_f32 = ffn_w_in.astype(F32), ffn_w_out.astype(F32)

    def ffn_weights(layer, half):
        return ((w_in_f32, (layer, half)), (w_out_f32, (layer, half)))

    rot_table, w_in, w_out = _rope_table(positions, cast=ffn_weights(0, 0))
    xt, w_in, w_out, w_glu = _ffn(
        x.astype(F32), gains[0, 0], w_in, w_out,
        relayout="bt_to_tb", cast_next=ffn_weights(0, 1) + ((s5_w_glu.astype(F32), (0,)),))
    wb, lam_tile, wc = _s5_weights(s5_lambda_re[0], s5_lambda_im[0], s5_log_step[0],
                                   s5_b_re[0], s5_b_im[0], s5_c_re[0], s5_c_im[0])
    xt = _s5(xt, gains[0, 1], wb, lam_tile, wc, s5_d[0].astype(F32).reshape(1, D_MODEL), w_glu)
    wd = jnp.concatenate([w_dkv[:, :KV_LORA], _with_rot_cols(w_dkv[:, KV_LORA:])], axis=-1)
    xb, c_kv, k_parts, w_in, w_out = _ffn(
        xt, gains[0, 2], w_in, w_out, relayout="tb_to_bt", tail="kv",
        tail_args=(kv_in_norm.astype(F32).reshape(1, D_MODEL), wd.astype(BF16),
                   kv_latent_norm.astype(F32).reshape(1, KV_LORA)),
        cast_next=ffn_weights(1, 0))
    xb = xb.reshape(rows, D_MODEL)

    xb, c_q, w_in, w_out, w_o = _ffn(
        xb, gains[1, 0], w_in, w_out, tail="q",
        tail_args=(gains[1, 1], mla_w_dq[0].astype(BF16),
                   mla_q_norm[0].astype(F32).reshape(1, Q_LORA)),
        cast_next=ffn_weights(1, 1) + ((mla_w_o.astype(F32), (0,)),))
    w_uq_h = mla_w_uq[0].reshape(Q_LORA, MLA_HEADS, NOPE_DIM + ROPE_DIM)
    wq = jnp.concatenate([w_uq_h[..., :NOPE_DIM], _with_rot_cols(w_uq_h[..., NOPE_DIM:])], axis=-1)
    wq = wq.transpose(1, 0, 2).astype(BF16)
    wkv = w_ukv.reshape(KV_LORA, MLA_HEADS, NOPE_DIM + V_DIM).transpose(1, 0, 2).astype(BF16)
    o = _attention(c_q.reshape(BATCH, SEQ, Q_LORA), c_kv, k_parts, rot_table, wq, wkv)
    (out,) = _ffn(xb, gains[1, 2], w_in, w_out,
                  attn=(o.reshape(rows, MLA_HEADS * V_DIM), w_o),
                  tail="final", tail_args=(final_norm.astype(F32).reshape(1, D_MODEL),))
    return out.reshape(BATCH, SEQ, D_MODEL).astype(x.dtype)
```
